```python
import math
import jax, jax.numpy as jnp
from jax import lax
import numpy as np

D_MODEL = 1024
BATCH = 2
SEQ = 8192
DEPTH = 1
DEC_BATCH = 4
DEC_SEQ = 4096
PAST_LEN = 128

HEAD_DIM = 64
NA_HEADS = 8
WG_HEADS = 8
WG_KV_HEADS = 2
WG_GROUP = WG_HEADS // WG_KV_HEADS
NA_WIDTH = NA_HEADS * HEAD_DIM
WG_WIDTH = WG_HEADS * HEAD_DIM
WG_KV_WIDTH = WG_KV_HEADS * HEAD_DIM
MIX_WIDTH = NA_WIDTH + WG_WIDTH
IN_WIDTH = 3 * NA_WIDTH + WG_WIDTH + 2 * WG_KV_WIDTH
D_FF = -(-8 * D_MODEL // (3 * 256)) * 256
GRID_W = 64
NA_WIN_ROWS = 8
NA_WIN_COLS = 16
N_META = 16
WINDOW = 128
BLOCK = 128
T5_BUCKETS = 32
T5_MAX_DIST = 128
EPS = 1e-6
NEG_INF = -1e30
SPLITS = (NA_WIDTH, 2 * NA_WIDTH, 3 * NA_WIDTH, 3 * NA_WIDTH + WG_WIDTH,
          3 * NA_WIDTH + WG_WIDTH + WG_KV_WIDTH)

kernel_name = "hybrid_na_swa_encoder"


def _rmsnorm(x, g):
    xf = x.astype(jnp.float32)
    y = xf * lax.rsqrt(jnp.mean(xf * xf, axis=-1, keepdims=True) + EPS)
    return (y * g.astype(jnp.float32)).astype(x.dtype)


def _t5_bucket(rel):
    half = T5_BUCKETS // 2
    max_exact = half // 2
    ret = jnp.where(rel > 0, half, 0)
    n = jnp.abs(rel)
    nf = jnp.maximum(n, 1).astype(jnp.float32)
    large = max_exact + (jnp.log(nf / max_exact) / math.log(T5_MAX_DIST / max_exact)
                         * (half - max_exact)).astype(jnp.int32)
    large = jnp.minimum(large, half - 1)
    return ret + jnp.where(n < max_exact, n, large)


def _softmax_with_sink(logits, sink):
    s = jnp.broadcast_to(sink, logits.shape[:-1] + (1,))
    p = jax.nn.softmax(jnp.concatenate([logits, s], axis=-1), axis=-1)
    return p[..., :-1]


def _neighbourhood_attn(q, k, v, rpb):
    B, L = q.shape[0], q.shape[1]
    N = L - N_META
    rows = N // GRID_W
    wr = min(NA_WIN_ROWS, rows)
    wc = NA_WIN_COLS
    scale = HEAD_DIM ** -0.5
    qm, km, vm = q[:, :N_META], k[:, :N_META], v[:, :N_META]
    qg = q[:, N_META:].reshape(B, rows, GRID_W, NA_HEADS, HEAD_DIM)
    kg = k[:, N_META:].reshape(B, rows, GRID_W, NA_HEADS, HEAD_DIM)
    vg = v[:, N_META:].reshape(B, rows, GRID_W, NA_HEADS, HEAD_DIM)
    row_start = jnp.clip(jnp.arange(rows) - wr // 2, 0, rows - wr)
    cols = jnp.arange(GRID_W)
    col_start = jnp.clip(cols - wc // 2, 0, GRID_W - wc)
    col_idx = col_start[:, None] + jnp.arange(wc)[None, :]
    dcol = col_idx - cols[:, None]
    rpb32 = rpb.astype(jnp.float32)

    def row_fn(r):
        rs = row_start[r]
        q_r = lax.dynamic_index_in_dim(qg, r, axis=1, keepdims=False)
        k_rows = lax.dynamic_slice_in_dim(kg, rs, wr, axis=1)
        v_rows = lax.dynamic_slice_in_dim(vg, rs, wr, axis=1)
        k_win = k_rows[:, :, col_idx]
        v_win = v_rows[:, :, col_idx]
        drow = rs + jnp.arange(wr) - r
        bias = rpb32[:, drow[None, :, None] + NA_WIN_ROWS - 1,
                     dcol[:, None, :] + NA_WIN_COLS - 1]
        s_win = jnp.einsum('bqhd,bwqchd->bhqwc', q_r, k_win).astype(jnp.float32) * scale + bias[None]
        s_meta = jnp.einsum('bqhd,bmhd->bhqm', q_r, km).astype(jnp.float32) * scale
        logits = jnp.concatenate([s_meta, s_win.reshape(B, NA_HEADS, GRID_W, wr * wc)], axis=-1)
        p = jax.nn.softmax(logits, axis=-1).astype(v.dtype)
        p_meta = p[..., :N_META]
        p_win = p[..., N_META:].reshape(B, NA_HEADS, GRID_W, wr, wc)
        return (jnp.einsum('bhqm,bmhd->bqhd', p_meta, vm)
                + jnp.einsum('bhqwc,bwqchd->bqhd', p_win, v_win))

    out = lax.map(row_fn, jnp.arange(rows))
    out = out.transpose(1, 0, 2, 3, 4).reshape(B, N, NA_WIDTH)
    s_m = jnp.einsum('bmhd,bnhd->bhmn', qm, km).astype(jnp.float32) * scale
    p_m = jax.nn.softmax(s_m, axis=-1).astype(v.dtype)
    out_m = jnp.einsum('bhmn,bnhd->bmhd', p_m, vm).reshape(B, N_META, NA_WIDTH)
    return jnp.concatenate([out_m, out], axis=1)


def _windowed_gqa(q, k, v, t5_table, sink):
    B, L = q.shape[0], q.shape[1]
    N = L - N_META
    nblk = N // BLOCK
    scale = HEAD_DIM ** -0.5
    t5 = t5_table.astype(jnp.float32)
    sink_b = sink.astype(jnp.float32).reshape(WG_KV_HEADS, WG_GROUP, 1, 1)
    qm, km, vm = q[:, :N_META], k[:, :N_META], v[:, :N_META]
    qr, kr, vr = q[:, N_META:], k[:, N_META:], v[:, N_META:]

    qb = qr.reshape(B, nblk, BLOCK, WG_KV_HEADS, WG_GROUP, HEAD_DIM)
    pad = ((0, 0), (BLOCK, BLOCK), (0, 0), (0, 0))
    kp = jnp.pad(kr, pad)
    vp = jnp.pad(vr, pad)
    band_idx = jnp.arange(nblk)[:, None] * BLOCK + jnp.arange(3 * BLOCK)[None, :]
    kb = kp[:, band_idx]
    vb = vp[:, band_idx]
    i = jnp.arange(BLOCK)
    j = jnp.arange(3 * BLOCK)
    rel_band = (j[None, :] - BLOCK) - i[:, None]
    s_idx = jnp.arange(nblk)[:, None, None] * BLOCK - BLOCK + j[None, None, :]
    mask = (s_idx >= 0) & (s_idx < N) & (jnp.abs(rel_band)[None] <= WINDOW)
    bias_band = t5[_t5_bucket(rel_band)].transpose(2, 0, 1).reshape(
        WG_KV_HEADS, WG_GROUP, BLOCK, 3 * BLOCK)
    s_band = jnp.einsum('bnihgd,bnjhd->bnhgij', qb, kb).astype(jnp.float32) * scale + bias_band
    s_band = jnp.where(mask[None, :, None, None], s_band, NEG_INF)
    t_pos = jnp.arange(nblk)[:, None] * BLOCK + i[None, :]
    rel_meta = jnp.arange(N_META)[None, None, :] - (N_META + t_pos[:, :, None])
    bias_meta = t5[_t5_bucket(rel_meta)].transpose(0, 3, 1, 2).reshape(
        nblk, WG_KV_HEADS, WG_GROUP, BLOCK, N_META)
    s_meta = jnp.einsum('bnihgd,bmhd->bnhgim', qb, km).astype(jnp.float32) * scale + bias_meta[None]
    p = _softmax_with_sink(jnp.concatenate([s_meta, s_band], axis=-1), sink_b).astype(v.dtype)
    out = (jnp.einsum('bnhgim,bmhd->bnihgd', p[..., :N_META], vm)
           + jnp.einsum('bnhgij,bnjhd->bnihgd', p[..., N_META:], vb))
    out = out.reshape(B, N, WG_WIDTH)

    qmg = qm.reshape(B, N_META, WG_KV_HEADS, WG_GROUP, HEAD_DIM)
    k_mq = jnp.concatenate([km, kr[:, :BLOCK]], axis=1)
    v_mq = jnp.concatenate([vm, vr[:, :BLOCK]], axis=1)
    qpos = jnp.arange(N_META)
    kpos = jnp.arange(N_META + BLOCK)
    rel_m = kpos[None, :] - qpos[:, None]
    mask_m = (kpos[None, :] < N_META) | (jnp.abs(rel_m) <= WINDOW)
    bias_m = t5[_t5_bucket(rel_m)].transpose(2, 0, 1).reshape(
        WG_KV_HEADS, WG_GROUP, N_META, N_META + BLOCK)
    s_m = jnp.einsum('bmhgd,bjhd->bhgmj', qmg, k_mq).astype(jnp.float32) * scale + bias_m
    s_m = jnp.where(mask_m, s_m, NEG_INF)
    p_m = _softmax_with_sink(s_m, sink_b).astype(v.dtype)
    out_m = jnp.einsum('bhgmj,bjhd->bmhgd', p_m, v_mq).reshape(B, N_META, WG_WIDTH)
    return jnp.concatenate([out_m, out], axis=1)


def _layer(x, t5_table, norm1_g, w_in, qn_a_g, kn_a_g, rpb_a, qn_b_g, kn_b_g, sink_b,
           outn_a_g, outn_b_g, w_out, norm2_g, w_gate, w_up, w_down):
    B, L = x.shape[0], x.shape[1]
    h = _rmsnorm(x, norm1_g)
    proj = h @ w_in
    qa, ka, va, qb, kb, vb = jnp.split(proj, SPLITS, axis=-1)
    qa = _rmsnorm(qa.reshape(B, L, NA_HEADS, HEAD_DIM), qn_a_g)
    ka = _rmsnorm(ka.reshape(B, L, NA_HEADS, HEAD_DIM), kn_a_g)
    va = va.reshape(B, L, NA_HEADS, HEAD_DIM)
    qb = _rmsnorm(qb.reshape(B, L, WG_HEADS, HEAD_DIM), qn_b_g)
    kb = _rmsnorm(kb.reshape(B, L, WG_KV_HEADS, HEAD_DIM), kn_b_g)
    vb = vb.reshape(B, L, WG_KV_HEADS, HEAD_DIM)
    oa = _neighbourhood_attn(qa, ka, va, rpb_a)
    ob = _windowed_gqa(qb, kb, vb, t5_table, sink_b)
    mix = jnp.concatenate([_rmsnorm(oa, outn_a_g), _rmsnorm(ob, outn_b_g)], axis=-1)
    x = x + mix @ w_out
    h = _rmsnorm(x, norm2_g)
    x = x + (jax.nn.silu(h @ w_gate) * (h @ w_up)) @ w_down
    return x


def _encode(x, meta_tokens, t5_table, norm1_g, w_in, qn_a_g, kn_a_g, rpb_a, qn_b_g, kn_b_g,
            sink_b, outn_a_g, outn_b_g, w_out, norm2_g, w_gate, w_up, w_down):
    B = x.shape[0]
    meta = jnp.broadcast_to(meta_tokens[None].astype(x.dtype), (B, N_META, x.shape[-1]))
    h = jnp.concatenate([meta, x], axis=1)
    for l in range(DEPTH):
        h = _layer(h, t5_table, norm1_g[l], w_in[l], qn_a_g[l], kn_a_g[l], rpb_a[l],
                   qn_b_g[l], kn_b_g[l], sink_b[l], outn_a_g[l], outn_b_g[l], w_out[l],
                   norm2_g[l], w_gate[l], w_up[l], w_down[l])
    return h[:, N_META:]


def setup_inputs(seed: int = 0) -> dict:
    key = jax.random.key(seed)
    ks = jax.random.split(key, 20)
    f32 = jnp.float32

    def nrm(k, shape, scale):
        return jax.random.normal(k, shape, f32) * scale

    def gain(k, shape):
        return 1.0 + 0.02 * jax.random.normal(k, shape, f32)

    return {
        "x_prompt": nrm(ks[0], (BATCH, SEQ, D_MODEL), 1.0),
        "x_sample": nrm(ks[1], (DEC_BATCH, DEC_SEQ, D_MODEL), 1.0),
        "meta_tokens": nrm(ks[2], (N_META, D_MODEL), 1.0),
        "t5_table": nrm(ks[3], (T5_BUCKETS, WG_HEADS), 0.1),
        "norm1_g": gain(ks[4], (DEPTH, D_MODEL)),
        "w_in": nrm(ks[5], (DEPTH, D_MODEL, IN_WIDTH), D_MODEL ** -0.5),
        "qn_a_g": gain(ks[6], (DEPTH, HEAD_DIM)),
        "kn_a_g": gain(ks[7], (DEPTH, HEAD_DIM)),
        "rpb_a": nrm(ks[8], (DEPTH, NA_HEADS, 2 * NA_WIN_ROWS - 1, 2 * NA_WIN_COLS - 1), 0.1),
        "qn_b_g": gain(ks[9], (DEPTH, HEAD_DIM)),
        "kn_b_g": gain(ks[10], (DEPTH, HEAD_DIM)),
        "sink_b": nrm(ks[11], (DEPTH, WG_HEADS), 0.5),
        "outn_a_g": gain(ks[12], (DEPTH, NA_WIDTH)),
        "outn_b_g": gain(ks[13], (DEPTH, WG_WIDTH)),
        "w_out": nrm(ks[14], (DEPTH, MIX_WIDTH, D_MODEL), MIX_WIDTH ** -0.5),
        "norm2_g": gain(ks[15], (DEPTH, D_MODEL)),
        "w_gate": nrm(ks[16], (DEPTH, D_MODEL, D_FF), D_MODEL ** -0.5),
        "w_up": nrm(ks[17], (DEPTH, D_MODEL, D_FF), D_MODEL ** -0.5),
        "w_down": nrm(ks[18], (DEPTH, D_FF, D_MODEL), D_FF ** -0.5),
    }


def reference(x_prompt, x_sample, meta_tokens, t5_table, norm1_g, w_in, qn_a_g, kn_a_g, rpb_a,
              qn_b_g, kn_b_g, sink_b, outn_a_g, outn_b_g, w_out, norm2_g, w_gate, w_up, w_down):
    y_prompt = _encode(x_prompt, meta_tokens, t5_table, norm1_g, w_in, qn_a_g, kn_a_g, rpb_a,
                       qn_b_g, kn_b_g, sink_b, outn_a_g, outn_b_g, w_out, norm2_g,
                       w_gate, w_up, w_down)
    y_sample = _encode(x_sample, meta_tokens, t5_table, norm1_g, w_in, qn_a_g, kn_a_g, rpb_a,
                       qn_b_g, kn_b_g, sink_b, outn_a_g, outn_b_g, w_out, norm2_g,
                       w_gate, w_up, w_down)
    return (y_prompt, y_sample)
```

```python
import functools
import math

import jax
import jax.numpy as jnp
from jax import lax
from jax.experimental import pallas as pl
from jax.experimental.pallas import tpu as pltpu

D_MODEL = 1024
HEAD_DIM = 64
NA_HEADS = 8
WG_HEADS = 8
WG_KV_HEADS = 2
WG_GROUP = WG_HEADS // WG_KV_HEADS
NA_WIDTH = NA_HEADS * HEAD_DIM
WG_WIDTH = WG_HEADS * HEAD_DIM
WG_KV_WIDTH = WG_KV_HEADS * HEAD_DIM
IN_WIDTH = 3 * NA_WIDTH + WG_WIDTH + 2 * WG_KV_WIDTH
D_FF = 2816
GRID_W = 64
NA_WIN_ROWS = 8
NA_WIN_COLS = 16
N_META = 16
WINDOW = 128
BLOCK = 128
T5_BUCKETS = 32
T5_MAX_DIST = 128
EPS = 1e-6
NEG_INF = -1e30

QA0, KA0, VA0 = 0, NA_WIDTH, 2 * NA_WIDTH
QB0 = 3 * NA_WIDTH
KB0 = QB0 + WG_WIDTH
VB0 = KB0 + WG_KV_WIDTH

NA_UNIT_ROWS = 4
NA_UNIT = NA_UNIT_ROWS * GRID_W
NA_KEY_UNITS = 3
NA_NEG_SLAB = 2 * NA_WIN_ROWS - 1

PROJ_TM = 512
FFN_TM = 256
META_PAD = 128

VMEM_LIMIT_BYTES = 56 * 1024 * 1024

_TN = (((0,), (0,)), ((), ()))
_NT = (((1,), (1,)), ((), ()))
_NN = (((1,), (0,)), ((), ()))


def _params(n_grid_dims):
    return pltpu.CompilerParams(
        dimension_semantics=("arbitrary",) * n_grid_dims,
        vmem_limit_bytes=VMEM_LIMIT_BYTES)


_PROJ_CHUNKS = (
    (QA0, NA_WIDTH, True),
    (KA0, NA_WIDTH, True),
    (VA0, NA_WIDTH, False),
    (QB0, WG_WIDTH, True),
    (KB0, WG_KV_WIDTH, True),
    (VB0, WG_KV_WIDTH, False),
)


def _in_proj_kernel(x_ref, g1_ref, w_ref, gcol_ref, o_ref):
    x = x_ref[0]
    ms = jnp.mean(x * x, axis=-1, keepdims=True)
    h = (x * lax.rsqrt(ms + EPS) * g1_ref[...]).astype(jnp.bfloat16)
    for row0, rows, normed in _PROJ_CHUNKS:
        p = lax.dot_general(w_ref[row0:row0 + rows, :], h, _NT,
                            preferred_element_type=jnp.float32)
        if not normed:
            o_ref[0, row0:row0 + rows, :] = p.astype(jnp.bfloat16)
            continue
        for i in range(rows // HEAD_DIM):
            blk = p[i * HEAD_DIM:(i + 1) * HEAD_DIM, :]
            ss = jnp.mean(blk * blk, axis=0, keepdims=True)
            r0 = row0 + i * HEAD_DIM
            y = blk * lax.rsqrt(ss + EPS) * gcol_ref[r0:r0 + HEAD_DIM, :]
            o_ref[0, r0:r0 + HEAD_DIM, :] = y.astype(jnp.bfloat16)


def _in_proj(x, g1, w_t, gcol, tm):
    b, n, _ = x.shape
    return pl.pallas_call(
        _in_proj_kernel,
        grid=(b, n // tm),
        in_specs=[
            pl.BlockSpec((1, tm, D_MODEL), lambda i, j: (i, j, 0)),
            pl.BlockSpec((1, D_MODEL), lambda i, j: (0, 0)),
            pl.BlockSpec((IN_WIDTH, D_MODEL), lambda i, j: (0, 0)),
            pl.BlockSpec((IN_WIDTH, 1), lambda i, j: (0, 0)),
        ],
        out_specs=pl.BlockSpec((1, IN_WIDTH, tm), lambda i, j: (i, 0, j)),
        out_shape=jax.ShapeDtypeStruct((b, IN_WIDTH, n), jnp.bfloat16),
        compiler_params=_params(2),
        name="in_proj",
    )(x, g1, w_t, gcol)


def _na_kernel(q_ref, k0_ref, k1_ref, k2_ref, v0_ref, v1_ref, v2_ref,
               km_ref, vm_ref, t_ref, g_ref, o_ref, acc_ref, *, rows):
    u = pl.program_id(1)
    n_units = rows // NA_UNIT_ROWS
    ku = jnp.clip(u - 1, 0, n_units - NA_KEY_UNITS)
    q_row0 = NA_UNIT_ROWS * u
    k_row0 = NA_UNIT_ROWS * ku
    left = lax.broadcasted_iota(jnp.int32, (GRID_W, 2 * GRID_W), 1) < GRID_W

    slab = {}
    for j in range(NA_KEY_UNITS * NA_UNIT_ROWS):
        kr = k_row0 + j
        for rr in range(NA_UNIT_ROWS):
            r = q_row0 + rr
            rs = jnp.clip(r - NA_WIN_ROWS // 2, 0, rows - NA_WIN_ROWS)
            valid = (kr >= rs) & (kr < rs + NA_WIN_ROWS)
            slab[j, rr] = jnp.where(valid, kr - r + NA_WIN_ROWS - 1, NA_NEG_SLAB)

    k_refs = (k0_ref, k1_ref, k2_ref)
    v_refs = (v0_ref, v1_ref, v2_ref)
    for h in range(NA_HEADS):
        hs = slice(h * HEAD_DIM, (h + 1) * HEAD_DIM)
        q = q_ref[0, hs, :]
        s_blocks = []
        for c in range(NA_KEY_UNITS):
            s = lax.dot_general(k_refs[c][0, hs, :], q, _TN,
                                preferred_element_type=jnp.float32)
            bias_rows = []
            for jj in range(NA_UNIT_ROWS):
                j = NA_UNIT_ROWS * c + jj
                pairs = []
                for pp in range(NA_UNIT_ROWS // 2):
                    t_l = t_ref[h, slab[j, 2 * pp]]
                    t_r = t_ref[h, slab[j, 2 * pp + 1]]
                    pairs.append(jnp.where(left, t_l, t_r))
                bias_rows.append(jnp.concatenate(pairs, axis=1))
            s_blocks.append(s + jnp.concatenate(bias_rows, axis=0))
        s_meta = lax.dot_general(km_ref[hs, :], q, _TN,
                                 preferred_element_type=jnp.float32)

        m = jnp.max(s_meta, axis=0, keepdims=True)
        for s in s_blocks:
            m = jnp.maximum(m, jnp.max(s, axis=0, keepdims=True))
        p_meta = jnp.exp(s_meta - m)
        l = jnp.sum(p_meta, axis=0, keepdims=True)
        o = lax.dot_general(vm_ref[hs, :], p_meta.astype(jnp.bfloat16), _NN,
                            preferred_element_type=jnp.float32)
        for c in range(NA_KEY_UNITS):
            p = jnp.exp(s_blocks[c] - m)
            l = l + jnp.sum(p, axis=0, keepdims=True)
            o = o + lax.dot_general(v_refs[c][0, hs, :], p.astype(jnp.bfloat16), _NN,
                                    preferred_element_type=jnp.float32)
        acc_ref[hs, :] = o * (1.0 / l)

    full = acc_ref[...]
    ms = jnp.mean(full * full, axis=0, keepdims=True)
    o_ref[0] = (full * lax.rsqrt(ms + EPS) * g_ref[...]).astype(jnp.bfloat16)


def _na_attn(proj_t, km_t, vm_t, bias_slabs, g_col):
    b, _, n = proj_t.shape
    rows = n // GRID_W
    n_units = rows // NA_UNIT_ROWS
    assert rows % NA_UNIT_ROWS == 0 and n_units >= NA_KEY_UNITS

    def q_map(i, u):
        return (i, QA0 // NA_WIDTH, u)

    def kv_map(row_block, c):
        def index_map(i, u):
            return (i, row_block, jnp.clip(u - 1, 0, n_units - NA_KEY_UNITS) + c)
        return index_map

    blk = (1, NA_WIDTH, NA_UNIT)
    in_specs = [pl.BlockSpec(blk, q_map)]
    in_specs += [pl.BlockSpec(blk, kv_map(KA0 // NA_WIDTH, c)) for c in range(NA_KEY_UNITS)]
    in_specs += [pl.BlockSpec(blk, kv_map(VA0 // NA_WIDTH, c)) for c in range(NA_KEY_UNITS)]
    in_specs += [
        pl.BlockSpec((NA_WIDTH, N_META), lambda i, u: (0, 0)),
        pl.BlockSpec((NA_WIDTH, N_META), lambda i, u: (0, 0)),
        pl.BlockSpec(bias_slabs.shape, lambda i, u: (0, 0, 0, 0)),
        pl.BlockSpec((NA_WIDTH, 1), lambda i, u: (0, 0)),
    ]
    return pl.pallas_call(
        functools.partial(_na_kernel, rows=rows),
        grid=(b, n_units),
        in_specs=in_specs,
        out_specs=pl.BlockSpec(blk, lambda i, u: (i, 0, u)),
        out_shape=jax.ShapeDtypeStruct((b, NA_WIDTH, n), jnp.bfloat16),
        scratch_shapes=[pltpu.VMEM((NA_WIDTH, NA_UNIT), jnp.float32)],
        compiler_params=_params(2),
        name="na_attn",
    )(proj_t, proj_t, proj_t, proj_t, proj_t, proj_t, proj_t,
      km_t, vm_t, bias_slabs, g_col)


def _wg_kernel(q_ref, k0_ref, k1_ref, k2_ref, v0_ref, v1_ref, v2_ref,
               km_ref, vm_ref, band_ref, bmeta_ref, sink_ref, g_ref,
               o_ref, acc_ref, *, nblk):
    n = pl.program_id(1)
    pen = (jnp.where(n == 0, NEG_INF, 0.0), 0.0,
           jnp.where(n == nblk - 1, NEG_INF, 0.0))
    k_refs = (k0_ref, k1_ref, k2_ref)
    v_refs = (v0_ref, v1_ref, v2_ref)
    for g in range(WG_KV_HEADS):
        gs = slice(g * HEAD_DIM, (g + 1) * HEAD_DIM)
        q = jnp.concatenate(
            [q_ref[0, (g * WG_GROUP + hh) * HEAD_DIM:(g * WG_GROUP + hh + 1) * HEAD_DIM, :]
             for hh in range(WG_GROUP)], axis=1)
        s_blocks = []
        for c in range(3):
            s = lax.dot_general(k_refs[c][0, gs, :], q, _TN,
                                preferred_element_type=jnp.float32)
            s_blocks.append(s + band_ref[g, c * BLOCK:(c + 1) * BLOCK, :] + pen[c])
        s_meta = lax.dot_general(km_ref[gs, :], q, _TN,
                                 preferred_element_type=jnp.float32) + bmeta_ref[g, 0]

        sink = sink_ref[g]
        m = jnp.maximum(sink, jnp.max(s_meta, axis=0, keepdims=True))
        for s in s_blocks:
            m = jnp.maximum(m, jnp.max(s, axis=0, keepdims=True))
        p_meta = jnp.exp(s_meta - m)
        l = jnp.exp(sink - m) + jnp.sum(p_meta, axis=0, keepdims=True)
        o = lax.dot_general(vm_ref[gs, :], p_meta.astype(jnp.bfloat16), _NN,
                            preferred_element_type=jnp.float32)
        for c in range(3):
            p = jnp.exp(s_blocks[c] - m)
            l = l + jnp.sum(p, axis=0, keepdims=True)
            o = o + lax.dot_general(v_refs[c][0, gs, :], p.astype(jnp.bfloat16), _NN,
                                    preferred_element_type=jnp.float32)
        o = o * (1.0 / l)
        for hh in range(WG_GROUP):
            r0 = (g * WG_GROUP + hh) * HEAD_DIM
            acc_ref[r0:r0 + HEAD_DIM, :] = o[:, hh * BLOCK:(hh + 1) * BLOCK]

    full = acc_ref[...]
    ms = jnp.mean(full * full, axis=0, keepdims=True)
    o_ref[0] = (full * lax.rsqrt(ms + EPS) * g_ref[...]).astype(jnp.bfloat16)


def _wg_attn(proj_t, km_t, vm_t, band, bmeta, sink_row, g_col):
    b, _, n = proj_t.shape
    nblk = n // BLOCK

    def kv_map(row_block, c):
        def index_map(i, j):
            return (i, row_block, jnp.clip(j - 1 + c, 0, nblk - 1))
        return index_map

    kv_blk = (1, WG_KV_WIDTH, BLOCK)
    in_specs = [pl.BlockSpec((1, WG_WIDTH, BLOCK), lambda i, j: (i, QB0 // WG_WIDTH, j))]
    in_specs += [pl.BlockSpec(kv_blk, kv_map(KB0 // WG_KV_WIDTH, c)) for c in range(3)]
    in_specs += [pl.BlockSpec(kv_blk, kv_map(VB0 // WG_KV_WIDTH, c)) for c in range(3)]
    in_specs += [
        pl.BlockSpec((WG_KV_WIDTH, N_META), lambda i, j: (0, 0)),
        pl.BlockSpec((WG_KV_WIDTH, N_META), lambda i, j: (0, 0)),
        pl.BlockSpec(band.shape, lambda i, j: (0, 0, 0)),
        pl.BlockSpec((WG_KV_HEADS, 1, N_META, WG_GROUP * BLOCK), lambda i, j: (0, j, 0, 0)),
        pl.BlockSpec(sink_row.shape, lambda i, j: (0, 0, 0)),
        pl.BlockSpec((WG_WIDTH, 1), lambda i, j: (0, 0)),
    ]
    return pl.pallas_call(
        functools.partial(_wg_kernel, nblk=nblk),
        grid=(b, nblk),
        in_specs=in_specs,
        out_specs=pl.BlockSpec((1, WG_WIDTH, BLOCK), lambda i, j: (i, 0, j)),
        out_shape=jax.ShapeDtypeStruct((b, WG_WIDTH, n), jnp.bfloat16),
        scratch_shapes=[pltpu.VMEM((WG_WIDTH, BLOCK), jnp.float32)],
        compiler_params=_params(2),
        name="wg_attn",
    )(proj_t, proj_t, proj_t, proj_t, proj_t, proj_t, proj_t,
      km_t, vm_t, band, bmeta, sink_row, g_col)


def _out_ffn_kernel(ma_ref, mb_ref, x_ref, woa_ref, wob_ref, g2_ref,
                    wg_ref, wu_ref, wd_ref, o_ref):
    attn = lax.dot_general(ma_ref[0], woa_ref[...], _TN,
                           preferred_element_type=jnp.float32)
    attn = attn + lax.dot_general(mb_ref[0], wob_ref[...], _TN,
                                  preferred_element_type=jnp.float32)
    x1 = x_ref[0] + attn
    ms = jnp.mean(x1 * x1, axis=-1, keepdims=True)
    h = (x1 * lax.rsqrt(ms + EPS) * g2_ref[...]).astype(jnp.bfloat16)
    gate = jnp.dot(h, wg_ref[...], preferred_element_type=jnp.float32)
    up = jnp.dot(h, wu_ref[...], preferred_element_type=jnp.float32)
    act = (gate * jax.nn.sigmoid(gate) * up).astype(jnp.bfloat16)
    o_ref[0] = x1 + jnp.dot(act, wd_ref[...], preferred_element_type=jnp.float32)


def _out_ffn(mix_a, mix_b, x, wo_a, wo_b, g2, w_gate, w_up, w_down, tm):
    b, n, _ = x.shape

    def const(shape):
        return pl.BlockSpec(shape, lambda i, j: (0,) * len(shape),
                            pipeline_mode=pl.Buffered(1))

    mix_spec = pl.BlockSpec((1, NA_WIDTH, tm), lambda i, j: (i, 0, j))
    x_spec = pl.BlockSpec((1, tm, D_MODEL), lambda i, j: (i, j, 0))
    return pl.pallas_call(
        _out_ffn_kernel,
        grid=(b, n // tm),
        in_specs=[mix_spec, mix_spec, x_spec,
                  const((NA_WIDTH, D_MODEL)), const((WG_WIDTH, D_MODEL)),
                  const((1, D_MODEL)),
                  const((D_MODEL, D_FF)), const((D_MODEL, D_FF)),
                  const((D_FF, D_MODEL))],
        out_specs=x_spec,
        out_shape=jax.ShapeDtypeStruct((b, n, D_MODEL), jnp.float32),
        compiler_params=_params(2),
        name="out_ffn",
    )(mix_a, mix_b, x, wo_a, wo_b, g2, w_gate, w_up, w_down)


def _t5_bucket(rel):
    half = T5_BUCKETS // 2
    max_exact = half // 2
    ret = jnp.where(rel > 0, half, 0)
    n = jnp.abs(rel)
    nf = jnp.maximum(n, 1).astype(jnp.float32)
    large = max_exact + (jnp.log(nf / max_exact) / math.log(T5_MAX_DIST / max_exact)
                         * (half - max_exact)).astype(jnp.int32)
    large = jnp.minimum(large, half - 1)
    return ret + jnp.where(n < max_exact, n, large)


def _na_bias_slabs(rpb):
    kc = jnp.arange(GRID_W)[:, None]
    c = jnp.arange(GRID_W)[None, :]
    cs = jnp.clip(c - NA_WIN_COLS // 2, 0, GRID_W - NA_WIN_COLS)
    in_win = (kc >= cs) & (kc < cs + NA_WIN_COLS)
    dcol = jnp.clip(kc - c + NA_WIN_COLS - 1, 0, 2 * NA_WIN_COLS - 2)
    t = jnp.where(in_win, rpb.astype(jnp.float32)[:, :, dcol], NEG_INF)
    t = jnp.concatenate([t, jnp.full_like(t[:, :1], NEG_INF)], axis=1)
    return jnp.concatenate([t, t], axis=-1)


def _wg_band_bias(t5):
    j = jnp.arange(3 * BLOCK)[:, None]
    i = jnp.arange(BLOCK)[None, :]
    rel = j - BLOCK - i
    tb = jnp.where((jnp.abs(rel) <= WINDOW)[..., None], t5[_t5_bucket(rel)], NEG_INF)
    tb = tb.transpose(2, 0, 1).reshape(WG_KV_HEADS, WG_GROUP, 3 * BLOCK, BLOCK)
    return tb.transpose(0, 2, 1, 3).reshape(WG_KV_HEADS, 3 * BLOCK, WG_GROUP * BLOCK)


def _wg_meta_bias(t5, n):
    nblk = n // BLOCK
    rel = jnp.arange(N_META)[None, :] - (N_META + jnp.arange(n)[:, None])
    bm = t5[_t5_bucket(rel)]
    bm = bm.reshape(nblk, BLOCK, N_META, WG_KV_HEADS, WG_GROUP)
    return bm.transpose(3, 0, 2, 4, 1).reshape(WG_KV_HEADS, nblk, N_META, WG_GROUP * BLOCK)


def _encode(x, meta_proj, w_in_t, g1, gcol, na_slabs, ga_col, band, sink_row, gb_col,
            t5, wo_a, wo_b, g2, w_gate, w_up, w_down):
    n = x.shape[1]
    proj_t = _in_proj(x, g1, w_in_t, gcol, PROJ_TM)
    mix_a = _na_attn(proj_t, meta_proj[KA0:KA0 + NA_WIDTH], meta_proj[VA0:VA0 + NA_WIDTH],
                     na_slabs, ga_col)
    mix_b = _wg_attn(proj_t, meta_proj[KB0:KB0 + WG_KV_WIDTH], meta_proj[VB0:VB0 + WG_KV_WIDTH],
                     band, _wg_meta_bias(t5, n), sink_row, gb_col)
    return _out_ffn(mix_a, mix_b, x, wo_a, wo_b, g2, w_gate, w_up, w_down, FFN_TM)


def kernel(x_prompt, x_sample, meta_tokens, t5_table, norm1_g, w_in, qn_a_g, kn_a_g, rpb_a,
           qn_b_g, kn_b_g, sink_b, outn_a_g, outn_b_g, w_out, norm2_g, w_gate, w_up, w_down):
    f32, bf16 = jnp.float32, jnp.bfloat16
    scale = HEAD_DIM ** -0.5
    w_in_t = w_in[0].T.astype(bf16)
    g1 = norm1_g[0].astype(f32).reshape(1, D_MODEL)
    g2 = norm2_g[0].astype(f32).reshape(1, D_MODEL)
    ones = jnp.ones
    gcol = jnp.concatenate([
        jnp.tile(qn_a_g[0].astype(f32), NA_HEADS) * scale,
        jnp.tile(kn_a_g[0].astype(f32), NA_HEADS),
        ones((NA_WIDTH,), f32),
        jnp.tile(qn_b_g[0].astype(f32), WG_HEADS) * scale,
        jnp.tile(kn_b_g[0].astype(f32), WG_KV_HEADS),
        ones((WG_KV_WIDTH,), f32),
    ]).reshape(IN_WIDTH, 1)
    ga_col = outn_a_g[0].astype(f32).reshape(NA_WIDTH, 1)
    gb_col = outn_b_g[0].astype(f32).reshape(WG_WIDTH, 1)
    t5 = t5_table.astype(f32)
    na_slabs = _na_bias_slabs(rpb_a[0])
    band = _wg_band_bias(t5)
    sink_row = jnp.repeat(sink_b[0].astype(f32).reshape(WG_KV_HEADS, WG_GROUP), BLOCK,
                          axis=1).reshape(WG_KV_HEADS, 1, WG_GROUP * BLOCK)
    wo = w_out[0].astype(bf16)
    wo_a, wo_b = wo[:NA_WIDTH], wo[NA_WIDTH:]
    wg, wu, wd = w_gate[0].astype(bf16), w_up[0].astype(bf16), w_down[0].astype(bf16)

    meta = jnp.zeros((1, META_PAD, D_MODEL), f32).at[0, :N_META].set(meta_tokens.astype(f32))
    meta_proj = _in_proj(meta, g1, w_in_t, gcol, META_PAD)[0, :, :N_META]

    def enc(x):
        return _encode(x, meta_proj, w_in_t, g1, gcol, na_slabs, ga_col, band, sink_row,
                       gb_col, t5, wo_a, wo_b, g2, wg, wu, wd)

    return (enc(x_prompt), enc(x_sample))
```

```python
import functools
import math

import jax
import jax.numpy as jnp
from jax import lax
from jax.experimental import pallas as pl
from jax.experimental.pallas import tpu as pltpu

D_MODEL = 1024
HEAD_DIM = 64
NA_HEADS = 8
WG_HEADS = 8
WG_KV_HEADS = 2
WG_GROUP = WG_HEADS // WG_KV_HEADS
NA_WIDTH = NA_HEADS * HEAD_DIM
WG_WIDTH = WG_HEADS * HEAD_DIM
WG_KV_WIDTH = WG_KV_HEADS * HEAD_DIM
IN_WIDTH = 3 * NA_WIDTH + WG_WIDTH + 2 * WG_KV_WIDTH
D_FF = 2816
GRID_W = 64
NA_WIN_ROWS = 8
NA_WIN_COLS = 16
N_META = 16
WINDOW = 128
BLOCK = 128
T5_BUCKETS = 32
T5_MAX_DIST = 128
EPS = 1e-6
NEG_INF = -1e30

QA0, KA0, VA0 = 0, NA_WIDTH, 2 * NA_WIDTH
QB0 = 3 * NA_WIDTH
KB0 = QB0 + WG_WIDTH
VB0 = KB0 + WG_KV_WIDTH

NA_UNIT_ROWS = 4
NA_UNIT = NA_UNIT_ROWS * GRID_W
NA_KEY_UNITS = 3
NA_NEG_SLAB = 2 * NA_WIN_ROWS - 1

PROJ_TM = 512
FFN_TM = 256
META_PAD = 128

VMEM_LIMIT_BYTES = 56 * 1024 * 1024

_TN = (((0,), (0,)), ((), ()))
_NT = (((1,), (1,)), ((), ()))
_NN = (((1,), (0,)), ((), ()))


def _params(n_grid_dims):
    return pltpu.CompilerParams(
        dimension_semantics=("arbitrary",) * n_grid_dims,
        vmem_limit_bytes=VMEM_LIMIT_BYTES)


_PROJ_CHUNKS = (
    (QA0, NA_WIDTH, True),
    (KA0, NA_WIDTH, True),
    (VA0, NA_WIDTH, False),
    (QB0, WG_WIDTH, True),
    (KB0, WG_KV_WIDTH, True),
    (VB0, WG_KV_WIDTH, False),
)


def _in_proj_kernel(x_ref, g1_ref, w_ref, gcol_ref, o_ref):
    x = x_ref[0]
    ms = jnp.mean(x * x, axis=-1, keepdims=True)
    h = (x * lax.rsqrt(ms + EPS) * g1_ref[...]).astype(jnp.bfloat16)
    for row0, rows, normed in _PROJ_CHUNKS:
        p = lax.dot_general(w_ref[row0:row0 + rows, :], h, _NT,
                            preferred_element_type=jnp.float32)
        if not normed:
            o_ref[0, row0:row0 + rows, :] = p.astype(jnp.bfloat16)
            continue
        for i in range(rows // HEAD_DIM):
            blk = p[i * HEAD_DIM:(i + 1) * HEAD_DIM, :]
            ss = jnp.mean(blk * blk, axis=0, keepdims=True)
            r0 = row0 + i * HEAD_DIM
            y = blk * lax.rsqrt(ss + EPS) * gcol_ref[r0:r0 + HEAD_DIM, :]
            o_ref[0, r0:r0 + HEAD_DIM, :] = y.astype(jnp.bfloat16)


def _in_proj(x, g1, w_t, gcol, tm):
    b, n, _ = x.shape
    return pl.pallas_call(
        _in_proj_kernel,
        grid=(b, n // tm),
        in_specs=[
            pl.BlockSpec((1, tm, D_MODEL), lambda i, j: (i, j, 0)),
            pl.BlockSpec((1, D_MODEL), lambda i, j: (0, 0)),
            pl.BlockSpec((IN_WIDTH, D_MODEL), lambda i, j: (0, 0)),
            pl.BlockSpec((IN_WIDTH, 1), lambda i, j: (0, 0)),
        ],
        out_specs=pl.BlockSpec((1, IN_WIDTH, tm), lambda i, j: (i, 0, j)),
        out_shape=jax.ShapeDtypeStruct((b, IN_WIDTH, n), jnp.bfloat16),
        compiler_params=_params(2),
        name="in_proj",
    )(x, g1, w_t, gcol)


def _na_kernel(q_ref, k0_ref, k1_ref, k2_ref, v0_ref, v1_ref, v2_ref,
               km_ref, vm_ref, t_ref, g_ref, o_ref, acc_ref, *, rows):
    u = pl.program_id(1)
    n_units = rows // NA_UNIT_ROWS
    ku = jnp.clip(u - 1, 0, n_units - NA_KEY_UNITS)
    q_row0 = NA_UNIT_ROWS * u
    k_row0 = NA_UNIT_ROWS * ku
    left = lax.broadcasted_iota(jnp.int32, (GRID_W, 2 * GRID_W), 1) < GRID_W

    slab = {}
    for j in range(NA_KEY_UNITS * NA_UNIT_ROWS):
        kr = k_row0 + j
        for rr in range(NA_UNIT_ROWS):
            r = q_row0 + rr
            rs = jnp.clip(r - NA_WIN_ROWS // 2, 0, rows - NA_WIN_ROWS)
            valid = (kr >= rs) & (kr < rs + NA_WIN_ROWS)
            slab[j, rr] = jnp.where(valid, kr - r + NA_WIN_ROWS - 1, NA_NEG_SLAB)

    k_refs = (k0_ref, k1_ref, k2_ref)
    v_refs = (v0_ref, v1_ref, v2_ref)
    for h in range(NA_HEADS):
        hs = slice(h * HEAD_DIM, (h + 1) * HEAD_DIM)
        q = q_ref[0, hs, :]
        s_blocks = []
        for c in range(NA_KEY_UNITS):
            s = lax.dot_general(k_refs[c][0, hs, :], q, _TN,
                                preferred_element_type=jnp.float32)
            bias_rows = []
            for jj in range(NA_UNIT_ROWS):
                j = NA_UNIT_ROWS * c + jj
                pairs = []
                for pp in range(NA_UNIT_ROWS // 2):
                    t_l = t_ref[h, slab[j, 2 * pp]]
                    t_r = t_ref[h, slab[j, 2 * pp + 1]]
                    pairs.append(jnp.where(left, t_l, t_r))
                bias_rows.append(jnp.concatenate(pairs, axis=1))
            s_blocks.append(s + jnp.concatenate(bias_rows, axis=0))
        s_meta = lax.dot_general(km_ref[hs, :], q, _TN,
                                 preferred_element_type=jnp.float32)

        m = jnp.max(s_meta, axis=0, keepdims=True)
        for s in s_blocks:
            m = jnp.maximum(m, jnp.max(s, axis=0, keepdims=True))
        p_meta = jnp.exp(s_meta - m)
        l = jnp.sum(p_meta, axis=0, keepdims=True)
        o = lax.dot_general(vm_ref[hs, :], p_meta.astype(jnp.bfloat16), _NN,
                            preferred_element_type=jnp.float32)
        for c in range(NA_KEY_UNITS):
            p = jnp.exp(s_blocks[c] - m)
            l = l + jnp.sum(p, axis=0, keepdims=True)
            o = o + lax.dot_general(v_refs[c][0, hs, :], p.astype(jnp.bfloat16), _NN,
                                    preferred_element_type=jnp.float32)
        acc_ref[hs, :] = o * (1.0 / l)

    full = acc_ref[...]
    ms = jnp.mean(full * full, axis=0, keepdims=True)
    o_ref[0] = (full * lax.rsqrt(ms + EPS) * g_ref[...]).astype(jnp.bfloat16)


def _na_attn(proj_t, km_t, vm_t, bias_slabs, g_col):
    b, _, n = proj_t.shape
    rows = n // GRID_W
    n_units = rows // NA_UNIT_ROWS
    assert rows % NA_UNIT_ROWS == 0 and n_units >= NA_KEY_UNITS

    def q_map(i, u):
        return (i, QA0 // NA_WIDTH, u)

    def kv_map(row_block, c):
        def index_map(i, u):
            return (i, row_block, jnp.clip(u - 1, 0, n_units - NA_KEY_UNITS) + c)
        return index_map

    blk = (1, NA_WIDTH, NA_UNIT)
    in_specs = [pl.BlockSpec(blk, q_map)]
    in_specs += [pl.BlockSpec(blk, kv_map(KA0 // NA_WIDTH, c)) for c in range(NA_KEY_UNITS)]
    in_specs += [pl.BlockSpec(blk, kv_map(VA0 // NA_WIDTH, c)) for c in range(NA_KEY_UNITS)]
    in_specs += [
        pl.BlockSpec((NA_WIDTH, N_META), lambda i, u: (0, 0)),
        pl.BlockSpec((NA_WIDTH, N_META), lambda i, u: (0, 0)),
        pl.BlockSpec(bias_slabs.shape, lambda i, u: (0, 0, 0, 0)),
        pl.BlockSpec((NA_WIDTH, 1), lambda i, u: (0, 0)),
    ]
    return pl.pallas_call(
        functools.partial(_na_kernel, rows=rows),
        grid=(b, n_units),
        in_specs=in_specs,
        out_specs=pl.BlockSpec(blk, lambda i, u: (i, 0, u)),
        out_shape=jax.ShapeDtypeStruct((b, NA_WIDTH, n), jnp.bfloat16),
        scratch_shapes=[pltpu.VMEM((NA_WIDTH, NA_UNIT), jnp.float32)],
        compiler_params=_params(2),
        name="na_attn",
    )(proj_t, proj_t, proj_t, proj_t, proj_t, proj_t, proj_t,
      km_t, vm_t, bias_slabs, g_col)


def _wg_kernel(q_ref, k0_ref, k1_ref, k2_ref, v0_ref, v1_ref, v2_ref,
               km_ref, vm_ref, band_ref, bmeta_ref, sink_ref, g_ref,
               o_ref, acc_ref, *, nblk):
    n = pl.program_id(1)
    pen = (jnp.where(n == 0, NEG_INF, 0.0), 0.0,
           jnp.where(n == nblk - 1, NEG_INF, 0.0))
    k_refs = (k0_ref, k1_ref, k2_ref)
    v_refs = (v0_ref, v1_ref, v2_ref)
    for g in range(WG_KV_HEADS):
        gs = slice(g * HEAD_DIM, (g + 1) * HEAD_DIM)
        q = jnp.concatenate(
            [q_ref[0, (g * WG_GROUP + hh) * HEAD_DIM:(g * WG_GROUP + hh + 1) * HEAD_DIM, :]
             for hh in range(WG_GROUP)], axis=1)
        s_blocks = []
        for c in range(3):
            s = lax.dot_general(k_refs[c][0, gs, :], q, _TN,
                                preferred_element_type=jnp.float32)
            s_blocks.append(s + band_ref[g, c * BLOCK:(c + 1) * BLOCK, :] + pen[c])
        s_meta = lax.dot_general(km_ref[gs, :], q, _TN,
                                 preferred_element_type=jnp.float32) + bmeta_ref[g, 0]

        sink = sink_ref[g]
        m = jnp.maximum(sink, jnp.max(s_meta, axis=0, keepdims=True))
        for s in s_blocks:
            m = jnp.maximum(m, jnp.max(s, axis=0, keepdims=True))
        p_meta = jnp.exp(s_meta - m)
        l = jnp.exp(sink - m) + jnp.sum(p_meta, axis=0, keepdims=True)
        o = lax.dot_general(vm_ref[gs, :], p_meta.astype(jnp.bfloat16), _NN,
                            preferred_element_type=jnp.float32)
        for c in range(3):
            p = jnp.exp(s_blocks[c] - m)
            l = l + jnp.sum(p, axis=0, keepdims=True)
            o = o + lax.dot_general(v_refs[c][0, gs, :], p.astype(jnp.bfloat16), _NN,
                                    preferred_element_type=jnp.float32)
        o = o * (1.0 / l)
        for hh in range(WG_GROUP):
            r0 = (g * WG_GROUP + hh) * HEAD_DIM
            acc_ref[r0:r0 + HEAD_DIM, :] = o[:, hh * BLOCK:(hh + 1) * BLOCK]

    full = acc_ref[...]
    ms = jnp.mean(full * full, axis=0, keepdims=True)
    o_ref[0] = (full * lax.rsqrt(ms + EPS) * g_ref[...]).astype(jnp.bfloat16)


def _wg_attn(proj_t, km_t, vm_t, band, bmeta, sink_row, g_col):
    b, _, n = proj_t.shape
    nblk = n // BLOCK

    def kv_map(row_block, c):
        def index_map(i, j):
            return (i, row_block, jnp.clip(j - 1 + c, 0, nblk - 1))
        return index_map

    kv_blk = (1, WG_KV_WIDTH, BLOCK)
    in_specs = [pl.BlockSpec((1, WG_WIDTH, BLOCK), lambda i, j: (i, QB0 // WG_WIDTH, j))]
    in_specs += [pl.BlockSpec(kv_blk, kv_map(KB0 // WG_KV_WIDTH, c)) for c in range(3)]
    in_specs += [pl.BlockSpec(kv_blk, kv_map(VB0 // WG_KV_WIDTH, c)) for c in range(3)]
    in_specs += [
        pl.BlockSpec((WG_KV_WIDTH, N_META), lambda i, j: (0, 0)),
        pl.BlockSpec((WG_KV_WIDTH, N_META), lambda i, j: (0, 0)),
        pl.BlockSpec(band.shape, lambda i, j: (0, 0, 0)),
        pl.BlockSpec((WG_KV_HEADS, 1, N_META, WG_GROUP * BLOCK),
                     lambda i, j: (0, jnp.minimum(j, 1), 0, 0)),
        pl.BlockSpec(sink_row.shape, lambda i, j: (0, 0, 0)),
        pl.BlockSpec((WG_WIDTH, 1), lambda i, j: (0, 0)),
    ]
    return pl.pallas_call(
        functools.partial(_wg_kernel, nblk=nblk),
        grid=(b, nblk),
        in_specs=in_specs,
        out_specs=pl.BlockSpec((1, WG_WIDTH, BLOCK), lambda i, j: (i, 0, j)),
        out_shape=jax.ShapeDtypeStruct((b, WG_WIDTH, n), jnp.bfloat16),
        scratch_shapes=[pltpu.VMEM((WG_WIDTH, BLOCK), jnp.float32)],
        compiler_params=_params(2),
        name="wg_attn",
    )(proj_t, proj_t, proj_t, proj_t, proj_t, proj_t, proj_t,
      km_t, vm_t, band, bmeta, sink_row, g_col)


def _out_ffn_kernel(ma_ref, mb_ref, x_ref, woa_ref, wob_ref, g2_ref,
                    wg_ref, wu_ref, wd_ref, o_ref):
    attn = lax.dot_general(ma_ref[0], woa_ref[...], _TN,
                           preferred_element_type=jnp.float32)
    attn = attn + lax.dot_general(mb_ref[0], wob_ref[...], _TN,
                                  preferred_element_type=jnp.float32)
    x1 = x_ref[0] + attn
    ms = jnp.mean(x1 * x1, axis=-1, keepdims=True)
    h = (x1 * lax.rsqrt(ms + EPS) * g2_ref[...]).astype(jnp.bfloat16)
    gate = jnp.dot(h, wg_ref[...], preferred_element_type=jnp.float32)
    up = jnp.dot(h, wu_ref[...], preferred_element_type=jnp.float32)
    act = (gate * jax.nn.sigmoid(gate) * up).astype(jnp.bfloat16)
    o_ref[0] = x1 + jnp.dot(act, wd_ref[...], preferred_element_type=jnp.float32)


def _out_ffn(mix_a, mix_b, x, wo_a, wo_b, g2, w_gate, w_up, w_down, tm):
    b, n, _ = x.shape

    def const(shape):
        return pl.BlockSpec(shape, lambda i, j: (0,) * len(shape),
                            pipeline_mode=pl.Buffered(1))

    mix_spec = pl.BlockSpec((1, NA_WIDTH, tm), lambda i, j: (i, 0, j))
    x_spec = pl.BlockSpec((1, tm, D_MODEL), lambda i, j: (i, j, 0))
    return pl.pallas_call(
        _out_ffn_kernel,
        grid=(b, n // tm),
        in_specs=[mix_spec, mix_spec, x_spec,
                  const((NA_WIDTH, D_MODEL)), const((WG_WIDTH, D_MODEL)),
                  const((1, D_MODEL)),
                  const((D_MODEL, D_FF)), const((D_MODEL, D_FF)),
                  const((D_FF, D_MODEL))],
        out_specs=x_spec,
        out_shape=jax.ShapeDtypeStruct((b, n, D_MODEL), jnp.float32),
        compiler_params=_params(2),
        name="out_ffn",
    )(mix_a, mix_b, x, wo_a, wo_b, g2, w_gate, w_up, w_down)


def _t5_bucket(rel):
    half = T5_BUCKETS // 2
    max_exact = half // 2
    ret = jnp.where(rel > 0, half, 0)
    n = jnp.abs(rel)
    nf = jnp.maximum(n, 1).astype(jnp.float32)
    large = max_exact + (jnp.log(nf / max_exact) / math.log(T5_MAX_DIST / max_exact)
                         * (half - max_exact)).astype(jnp.int32)
    large = jnp.minimum(large, half - 1)
    return ret + jnp.where(n < max_exact, n, large)


def _select_rows(table, idx):
    onehot = idx[..., None] == jnp.arange(table.shape[0])
    return jnp.sum(jnp.where(onehot[..., None], table, 0.0), axis=-2)


def _na_bias_slabs(rpb):
    kc = jnp.arange(GRID_W)[:, None]
    c = jnp.arange(GRID_W)[None, :]
    cs = jnp.clip(c - NA_WIN_COLS // 2, 0, GRID_W - NA_WIN_COLS)
    in_win = (kc >= cs) & (kc < cs + NA_WIN_COLS)
    dcol = jnp.clip(kc - c + NA_WIN_COLS - 1, 0, 2 * NA_WIN_COLS - 2)
    n_drow = 2 * NA_WIN_ROWS - 1
    by_dcol = rpb.astype(jnp.float32).reshape(NA_HEADS * n_drow, -1).T
    t = jnp.where(in_win[..., None], _select_rows(by_dcol, dcol), NEG_INF)
    t = t.transpose(2, 0, 1).reshape(NA_HEADS, n_drow, GRID_W, GRID_W)
    t = jnp.concatenate([t, jnp.full_like(t[:, :1], NEG_INF)], axis=1)
    return jnp.concatenate([t, t], axis=-1)


def _wg_band_bias(t5):
    j = jnp.arange(3 * BLOCK)[:, None]
    i = jnp.arange(BLOCK)[None, :]
    rel = j - BLOCK - i
    tb = jnp.where((jnp.abs(rel) <= WINDOW)[..., None], _select_rows(t5, _t5_bucket(rel)),
                   NEG_INF)
    tb = tb.transpose(2, 0, 1).reshape(WG_KV_HEADS, WG_GROUP, 3 * BLOCK, BLOCK)
    return tb.transpose(0, 2, 1, 3).reshape(WG_KV_HEADS, 3 * BLOCK, WG_GROUP * BLOCK)


def _wg_meta_bias(t5):
    assert BLOCK + 1 >= T5_MAX_DIST
    rel = jnp.arange(N_META)[None, :] - (N_META + jnp.arange(2 * BLOCK)[:, None])
    bm = _select_rows(t5, _t5_bucket(rel))
    bm = bm.reshape(2, BLOCK, N_META, WG_KV_HEADS, WG_GROUP)
    return bm.transpose(3, 0, 2, 4, 1).reshape(WG_KV_HEADS, 2, N_META, WG_GROUP * BLOCK)


def _encode(x, meta_proj, w_in_t, g1, gcol, na_slabs, ga_col, band, bmeta, sink_row, gb_col,
            wo_a, wo_b, g2, w_gate, w_up, w_down):
    proj_t = _in_proj(x, g1, w_in_t, gcol, PROJ_TM)
    mix_a = _na_attn(proj_t, meta_proj[KA0:KA0 + NA_WIDTH], meta_proj[VA0:VA0 + NA_WIDTH],
                     na_slabs, ga_col)
    mix_b = _wg_attn(proj_t, meta_proj[KB0:KB0 + WG_KV_WIDTH], meta_proj[VB0:VB0 + WG_KV_WIDTH],
                     band, bmeta, sink_row, gb_col)
    return _out_ffn(mix_a, mix_b, x, wo_a, wo_b, g2, w_gate, w_up, w_down, FFN_TM)


def kernel(x_prompt, x_sample, meta_tokens, t5_table, norm1_g, w_in, qn_a_g, kn_a_g, rpb_a,
           qn_b_g, kn_b_g, sink_b, outn_a_g, outn_b_g, w_out, norm2_g, w_gate, w_up, w_down):
    f32, bf16 = jnp.float32, jnp.bfloat16
    scale = HEAD_DIM ** -0.5
    w_in_t = w_in[0].T.astype(bf16)
    g1 = norm1_g[0].astype(f32).reshape(1, D_MODEL)
    g2 = norm2_g[0].astype(f32).reshape(1, D_MODEL)
    ones = jnp.ones
    gcol = jnp.concatenate([
        jnp.tile(qn_a_g[0].astype(f32), NA_HEADS) * scale,
        jnp.tile(kn_a_g[0].astype(f32), NA_HEADS),
        ones((NA_WIDTH,), f32),
        jnp.tile(qn_b_g[0].astype(f32), WG_HEADS) * scale,
        jnp.tile(kn_b_g[0].astype(f32), WG_KV_HEADS),
        ones((WG_KV_WIDTH,), f32),
    ]).reshape(IN_WIDTH, 1)
    ga_col = outn_a_g[0].astype(f32).reshape(NA_WIDTH, 1)
    gb_col = outn_b_g[0].astype(f32).reshape(WG_WIDTH, 1)
    t5 = t5_table.astype(f32)
    na_slabs = _na_bias_slabs(rpb_a[0])
    band = _wg_band_bias(t5)
    bmeta = _wg_meta_bias(t5)
    sink_row = jnp.repeat(sink_b[0].astype(f32).reshape(WG_KV_HEADS, WG_GROUP), BLOCK,
                          axis=1).reshape(WG_KV_HEADS, 1, WG_GROUP * BLOCK)
    wo = w_out[0].astype(bf16)
    wo_a, wo_b = wo[:NA_WIDTH], wo[NA_WIDTH:]
    wg, wu, wd = w_gate[0].astype(bf16), w_up[0].astype(bf16), w_down[0].astype(bf16)

    meta = jnp.zeros((1, META_PAD, D_MODEL), f32).at[0, :N_META].set(meta_tokens.astype(f32))
    meta_proj = _in_proj(meta, g1, w_in_t, gcol, META_PAD)[0, :, :N_META]

    def enc(x):
        return _encode(x, meta_proj, w_in_t, g1, gcol, na_slabs, ga_col, band, bmeta, sink_row,
                       gb_col, wo_a, wo_b, g2, wg, wu, wd)

    return (enc(x_prompt), enc(x_sample))
```

```python
import functools
import math

import jax
import jax.numpy as jnp
from jax import lax
from jax.experimental import pallas as pl
from jax.experimental.pallas import tpu as pltpu

D_MODEL = 1024
HEAD_DIM = 64
NA_HEADS = 8
WG_HEADS = 8
WG_KV_HEADS = 2
WG_GROUP = WG_HEADS // WG_KV_HEADS
NA_WIDTH = NA_HEADS * HEAD_DIM
WG_WIDTH = WG_HEADS * HEAD_DIM
WG_KV_WIDTH = WG_KV_HEADS * HEAD_DIM
IN_WIDTH = 3 * NA_WIDTH + WG_WIDTH + 2 * WG_KV_WIDTH
D_FF = 2816
GRID_W = 64
NA_WIN_ROWS = 8
NA_WIN_COLS = 16
N_META = 16
WINDOW = 128
BLOCK = 128
T5_BUCKETS = 32
T5_MAX_DIST = 128
EPS = 1e-6
NEG_INF = -1e30

QA0, KA0, VA0 = 0, NA_WIDTH, 2 * NA_WIDTH
QB0 = 3 * NA_WIDTH
KB0 = QB0 + WG_WIDTH
VB0 = KB0 + WG_KV_WIDTH

NA_UNIT_ROWS = 4
NA_UNIT = NA_UNIT_ROWS * GRID_W
NA_KEY_UNITS = 3
NA_NEG_SLAB = 2 * NA_WIN_ROWS - 1
WG_QB = 4

PROJ_TM = 512
FFN_TM = 256
META_PAD = 128

VMEM_LIMIT_BYTES = 56 * 1024 * 1024

_TN = (((0,), (0,)), ((), ()))
_NT = (((1,), (1,)), ((), ()))
_NN = (((1,), (0,)), ((), ()))


def _params(n_grid_dims):
    return pltpu.CompilerParams(
        dimension_semantics=("arbitrary",) * n_grid_dims,
        vmem_limit_bytes=VMEM_LIMIT_BYTES)


_PROJ_CHUNKS = (
    (QA0, NA_WIDTH, True),
    (KA0, NA_WIDTH, True),
    (VA0, NA_WIDTH, False),
    (QB0, WG_WIDTH, True),
    (KB0, WG_KV_WIDTH, True),
    (VB0, WG_KV_WIDTH, False),
)


def _in_proj_kernel(x_ref, g1_ref, w_ref, gcol_ref, o_ref):
    x = x_ref[0]
    ms = jnp.mean(x * x, axis=-1, keepdims=True)
    h = (x * lax.rsqrt(ms + EPS) * g1_ref[...]).astype(jnp.bfloat16)
    for row0, rows, normed in _PROJ_CHUNKS:
        p = lax.dot_general(w_ref[row0:row0 + rows, :], h, _NT,
                            preferred_element_type=jnp.float32)
        if not normed:
            o_ref[0, row0:row0 + rows, :] = p.astype(jnp.bfloat16)
            continue
        for i in range(rows // HEAD_DIM):
            blk = p[i * HEAD_DIM:(i + 1) * HEAD_DIM, :]
            ss = jnp.mean(blk * blk, axis=0, keepdims=True)
            r0 = row0 + i * HEAD_DIM
            y = blk * lax.rsqrt(ss + EPS) * gcol_ref[r0:r0 + HEAD_DIM, :]
            o_ref[0, r0:r0 + HEAD_DIM, :] = y.astype(jnp.bfloat16)


def _in_proj(x, g1, w_t, gcol, tm):
    b, n, _ = x.shape
    return pl.pallas_call(
        _in_proj_kernel,
        grid=(b, n // tm),
        in_specs=[
            pl.BlockSpec((1, tm, D_MODEL), lambda i, j: (i, j, 0)),
            pl.BlockSpec((1, D_MODEL), lambda i, j: (0, 0)),
            pl.BlockSpec((IN_WIDTH, D_MODEL), lambda i, j: (0, 0)),
            pl.BlockSpec((IN_WIDTH, 1), lambda i, j: (0, 0)),
        ],
        out_specs=pl.BlockSpec((1, IN_WIDTH, tm), lambda i, j: (i, 0, j)),
        out_shape=jax.ShapeDtypeStruct((b, IN_WIDTH, n), jnp.bfloat16),
        compiler_params=_params(2),
        name="in_proj",
    )(x, g1, w_t, gcol)


def _na_kernel(q_ref, k0_ref, k1_ref, k2_ref, v0_ref, v1_ref, v2_ref,
               km_ref, vm_ref, t_ref, g_ref, o_ref, acc_ref, *, rows):
    u = pl.program_id(1)
    n_units = rows // NA_UNIT_ROWS
    ku = jnp.clip(u - 1, 0, n_units - NA_KEY_UNITS)
    q_row0 = NA_UNIT_ROWS * u
    k_row0 = NA_UNIT_ROWS * ku
    left = lax.broadcasted_iota(jnp.int32, (GRID_W, 2 * GRID_W), 1) < GRID_W

    slab = {}
    for j in range(NA_KEY_UNITS * NA_UNIT_ROWS):
        kr = k_row0 + j
        for rr in range(NA_UNIT_ROWS):
            r = q_row0 + rr
            rs = jnp.clip(r - NA_WIN_ROWS // 2, 0, rows - NA_WIN_ROWS)
            valid = (kr >= rs) & (kr < rs + NA_WIN_ROWS)
            slab[j, rr] = jnp.where(valid, kr - r + NA_WIN_ROWS - 1, NA_NEG_SLAB)

    k_refs = (k0_ref, k1_ref, k2_ref)
    v_refs = (v0_ref, v1_ref, v2_ref)
    def scores(h):
        hs = slice(h * HEAD_DIM, (h + 1) * HEAD_DIM)
        q = q_ref[0, hs, :]
        s_blocks = []
        for c in range(NA_KEY_UNITS):
            s = lax.dot_general(k_refs[c][0, hs, :], q, _TN,
                                preferred_element_type=jnp.float32)
            bias_rows = []
            for jj in range(NA_UNIT_ROWS):
                j = NA_UNIT_ROWS * c + jj
                pairs = []
                for pp in range(NA_UNIT_ROWS // 2):
                    t_l = t_ref[h, slab[j, 2 * pp]]
                    t_r = t_ref[h, slab[j, 2 * pp + 1]]
                    pairs.append(jnp.where(left, t_l, t_r))
                bias_rows.append(jnp.concatenate(pairs, axis=1))
            s_blocks.append(s + jnp.concatenate(bias_rows, axis=0))
        s_meta = lax.dot_general(km_ref[hs, :], q, _TN,
                                 preferred_element_type=jnp.float32)
        return s_blocks, s_meta

    def finish(h, s_blocks, s_meta):
        hs = slice(h * HEAD_DIM, (h + 1) * HEAD_DIM)
        m = jnp.max(s_meta, axis=0, keepdims=True)
        for s in s_blocks:
            m = jnp.maximum(m, jnp.max(s, axis=0, keepdims=True))
        p_meta = jnp.exp(s_meta - m)
        l = jnp.sum(p_meta, axis=0, keepdims=True)
        o = lax.dot_general(vm_ref[hs, :], p_meta.astype(jnp.bfloat16), _NN,
                            preferred_element_type=jnp.float32)
        for c in range(NA_KEY_UNITS):
            p = jnp.exp(s_blocks[c] - m)
            l = l + jnp.sum(p, axis=0, keepdims=True)
            o = o + lax.dot_general(v_refs[c][0, hs, :], p.astype(jnp.bfloat16), _NN,
                                    preferred_element_type=jnp.float32)
        acc_ref[hs, :] = o * (1.0 / l)

    nxt = scores(0)
    for h in range(NA_HEADS):
        cur = nxt
        if h + 1 < NA_HEADS:
            nxt = scores(h + 1)
        finish(h, *cur)

    full = acc_ref[...]
    ms = jnp.mean(full * full, axis=0, keepdims=True)
    o_ref[0] = (full * lax.rsqrt(ms + EPS) * g_ref[...]).astype(jnp.bfloat16)


def _na_attn(proj_t, km_t, vm_t, bias_slabs, g_col):
    b, _, n = proj_t.shape
    rows = n // GRID_W
    n_units = rows // NA_UNIT_ROWS
    assert rows % NA_UNIT_ROWS == 0 and n_units >= NA_KEY_UNITS

    def q_map(i, u):
        return (i, QA0 // NA_WIDTH, u)

    def kv_map(row_block, c):
        def index_map(i, u):
            return (i, row_block, jnp.clip(u - 1, 0, n_units - NA_KEY_UNITS) + c)
        return index_map

    blk = (1, NA_WIDTH, NA_UNIT)
    in_specs = [pl.BlockSpec(blk, q_map)]
    in_specs += [pl.BlockSpec(blk, kv_map(KA0 // NA_WIDTH, c)) for c in range(NA_KEY_UNITS)]
    in_specs += [pl.BlockSpec(blk, kv_map(VA0 // NA_WIDTH, c)) for c in range(NA_KEY_UNITS)]
    in_specs += [
        pl.BlockSpec((NA_WIDTH, N_META), lambda i, u: (0, 0)),
        pl.BlockSpec((NA_WIDTH, N_META), lambda i, u: (0, 0)),
        pl.BlockSpec(bias_slabs.shape, lambda i, u: (0, 0, 0, 0)),
        pl.BlockSpec((NA_WIDTH, 1), lambda i, u: (0, 0)),
    ]
    return pl.pallas_call(
        functools.partial(_na_kernel, rows=rows),
        grid=(b, n_units),
        in_specs=in_specs,
        out_specs=pl.BlockSpec(blk, lambda i, u: (i, 0, u)),
        out_shape=jax.ShapeDtypeStruct((b, NA_WIDTH, n), jnp.bfloat16),
        scratch_shapes=[pltpu.VMEM((NA_WIDTH, NA_UNIT), jnp.float32)],
        compiler_params=_params(2),
        name="na_attn",
    )(proj_t, proj_t, proj_t, proj_t, proj_t, proj_t, proj_t,
      km_t, vm_t, bias_slabs, g_col)


def _wg_kernel(q_ref, kl_ref, km_ref, kr_ref, vl_ref, vm_ref, vr_ref,
               kmeta_ref, vmeta_ref, band_ref, bmeta_ref, sink_ref, g_ref,
               o_ref, acc_ref, *, n_steps):
    j = pl.program_id(1)
    first = j == 0
    last = j == n_steps - 1

    def piece(left_ref, main_ref, right_ref, gs, i):
        if i == 0:
            return left_ref[0, gs, :]
        if i == WG_QB + 1:
            return right_ref[0, gs, :]
        return main_ref[0, gs, (i - 1) * BLOCK:i * BLOCK]

    def scores(g, qb):
        gs = slice(g * HEAD_DIM, (g + 1) * HEAD_DIM)
        q = jnp.concatenate(
            [q_ref[0, (g * WG_GROUP + hh) * HEAD_DIM:(g * WG_GROUP + hh + 1) * HEAD_DIM,
                   qb * BLOCK:(qb + 1) * BLOCK]
             for hh in range(WG_GROUP)], axis=1)
        s_blocks = []
        for c in range(3):
            s = lax.dot_general(piece(kl_ref, km_ref, kr_ref, gs, qb + c), q, _TN,
                                preferred_element_type=jnp.float32)
            s = s + band_ref[g, c * BLOCK:(c + 1) * BLOCK, :]
            if c == 0 and qb == 0:
                s = s + jnp.where(first, NEG_INF, 0.0)
            if c == 2 and qb == WG_QB - 1:
                s = s + jnp.where(last, NEG_INF, 0.0)
            s_blocks.append(s)
        meta_slab = jnp.where(first, 0, 1) if qb == 0 else 1
        s_meta = lax.dot_general(kmeta_ref[gs, :], q, _TN,
                                 preferred_element_type=jnp.float32) + bmeta_ref[g, meta_slab]
        return s_blocks, s_meta

    def finish(g, qb, s_blocks, s_meta):
        gs = slice(g * HEAD_DIM, (g + 1) * HEAD_DIM)
        sink = sink_ref[g]
        m = jnp.maximum(sink, jnp.max(s_meta, axis=0, keepdims=True))
        for s in s_blocks:
            m = jnp.maximum(m, jnp.max(s, axis=0, keepdims=True))
        p_meta = jnp.exp(s_meta - m)
        l = jnp.exp(sink - m) + jnp.sum(p_meta, axis=0, keepdims=True)
        o = lax.dot_general(vmeta_ref[gs, :], p_meta.astype(jnp.bfloat16), _NN,
                            preferred_element_type=jnp.float32)
        for c in range(3):
            p = jnp.exp(s_blocks[c] - m)
            l = l + jnp.sum(p, axis=0, keepdims=True)
            o = o + lax.dot_general(piece(vl_ref, vm_ref, vr_ref, gs, qb + c),
                                    p.astype(jnp.bfloat16), _NN,
                                    preferred_element_type=jnp.float32)
        o = o * (1.0 / l)
        for hh in range(WG_GROUP):
            r0 = (g * WG_GROUP + hh) * HEAD_DIM
            acc_ref[r0:r0 + HEAD_DIM, qb * BLOCK:(qb + 1) * BLOCK] = (
                o[:, hh * BLOCK:(hh + 1) * BLOCK])

    units = [(g, qb) for qb in range(WG_QB) for g in range(WG_KV_HEADS)]
    nxt = scores(*units[0])
    for i, unit in enumerate(units):
        cur = nxt
        if i + 1 < len(units):
            nxt = scores(*units[i + 1])
        finish(*unit, *cur)

    full = acc_ref[...]
    ms = jnp.mean(full * full, axis=0, keepdims=True)
    o_ref[0] = (full * lax.rsqrt(ms + EPS) * g_ref[...]).astype(jnp.bfloat16)


def _wg_attn(proj_t, km_t, vm_t, band, bmeta, sink_row, g_col):
    b, _, n = proj_t.shape
    nblk = n // BLOCK
    n_steps = nblk // WG_QB
    assert nblk % WG_QB == 0

    def main_map(row_block):
        return lambda i, j: (i, row_block, j)

    def left_map(row_block):
        return lambda i, j: (i, row_block, jnp.maximum(j * WG_QB - 1, 0))

    def right_map(row_block):
        return lambda i, j: (i, row_block, jnp.minimum((j + 1) * WG_QB, nblk - 1))

    tq = WG_QB * BLOCK
    kb, vb = KB0 // WG_KV_WIDTH, VB0 // WG_KV_WIDTH
    side_blk = (1, WG_KV_WIDTH, BLOCK)
    main_blk = (1, WG_KV_WIDTH, tq)
    in_specs = [
        pl.BlockSpec((1, WG_WIDTH, tq), lambda i, j: (i, QB0 // WG_WIDTH, j)),
        pl.BlockSpec(side_blk, left_map(kb)), pl.BlockSpec(main_blk, main_map(kb)),
        pl.BlockSpec(side_blk, right_map(kb)),
        pl.BlockSpec(side_blk, left_map(vb)), pl.BlockSpec(main_blk, main_map(vb)),
        pl.BlockSpec(side_blk, right_map(vb)),
        pl.BlockSpec((WG_KV_WIDTH, N_META), lambda i, j: (0, 0)),
        pl.BlockSpec((WG_KV_WIDTH, N_META), lambda i, j: (0, 0)),
        pl.BlockSpec(band.shape, lambda i, j: (0, 0, 0)),
        pl.BlockSpec(bmeta.shape, lambda i, j: (0, 0, 0, 0)),
        pl.BlockSpec(sink_row.shape, lambda i, j: (0, 0, 0)),
        pl.BlockSpec((WG_WIDTH, 1), lambda i, j: (0, 0)),
    ]
    return pl.pallas_call(
        functools.partial(_wg_kernel, n_steps=n_steps),
        grid=(b, n_steps),
        in_specs=in_specs,
        out_specs=pl.BlockSpec((1, WG_WIDTH, tq), lambda i, j: (i, 0, j)),
        out_shape=jax.ShapeDtypeStruct((b, WG_WIDTH, n), jnp.bfloat16),
        scratch_shapes=[pltpu.VMEM((WG_WIDTH, tq), jnp.float32)],
        compiler_params=_params(2),
        name="wg_attn",
    )(proj_t, proj_t, proj_t, proj_t, proj_t, proj_t, proj_t,
      km_t, vm_t, band, bmeta, sink_row, g_col)


def _out_ffn_kernel(ma_ref, mb_ref, x_ref, woa_ref, wob_ref, g2_ref,
                    wg_ref, wu_ref, wd_ref, o_ref):
    attn = lax.dot_general(ma_ref[0], woa_ref[...], _TN,
                           preferred_element_type=jnp.float32)
    attn = attn + lax.dot_general(mb_ref[0], wob_ref[...], _TN,
                                  preferred_element_type=jnp.float32)
    x1 = x_ref[0] + attn
    ms = jnp.mean(x1 * x1, axis=-1, keepdims=True)
    h = (x1 * lax.rsqrt(ms + EPS) * g2_ref[...]).astype(jnp.bfloat16)
    gate = jnp.dot(h, wg_ref[...], preferred_element_type=jnp.float32)
    up = jnp.dot(h, wu_ref[...], preferred_element_type=jnp.float32)
    act = (gate * jax.nn.sigmoid(gate) * up).astype(jnp.bfloat16)
    o_ref[0] = x1 + jnp.dot(act, wd_ref[...], preferred_element_type=jnp.float32)


def _out_ffn(mix_a, mix_b, x, wo_a, wo_b, g2, w_gate, w_up, w_down, tm):
    b, n, _ = x.shape

    def const(shape):
        return pl.BlockSpec(shape, lambda i, j: (0,) * len(shape),
                            pipeline_mode=pl.Buffered(1))

    mix_spec = pl.BlockSpec((1, NA_WIDTH, tm), lambda i, j: (i, 0, j))
    x_spec = pl.BlockSpec((1, tm, D_MODEL), lambda i, j: (i, j, 0))
    return pl.pallas_call(
        _out_ffn_kernel,
        grid=(b, n // tm),
        in_specs=[mix_spec, mix_spec, x_spec,
                  const((NA_WIDTH, D_MODEL)), const((WG_WIDTH, D_MODEL)),
                  const((1, D_MODEL)),
                  const((D_MODEL, D_FF)), const((D_MODEL, D_FF)),
                  const((D_FF, D_MODEL))],
        out_specs=x_spec,
        out_shape=jax.ShapeDtypeStruct((b, n, D_MODEL), jnp.float32),
        compiler_params=_params(2),
        name="out_ffn",
    )(mix_a, mix_b, x, wo_a, wo_b, g2, w_gate, w_up, w_down)


def _t5_bucket(rel):
    half = T5_BUCKETS // 2
    max_exact = half // 2
    ret = jnp.where(rel > 0, half, 0)
    n = jnp.abs(rel)
    nf = jnp.maximum(n, 1).astype(jnp.float32)
    large = max_exact + (jnp.log(nf / max_exact) / math.log(T5_MAX_DIST / max_exact)
                         * (half - max_exact)).astype(jnp.int32)
    large = jnp.minimum(large, half - 1)
    return ret + jnp.where(n < max_exact, n, large)


def _select_rows(table, idx):
    onehot = idx[..., None] == jnp.arange(table.shape[0])
    return jnp.sum(jnp.where(onehot[..., None], table, 0.0), axis=-2)


def _na_bias_slabs(rpb):
    kc = jnp.arange(GRID_W)[:, None]
    c = jnp.arange(GRID_W)[None, :]
    cs = jnp.clip(c - NA_WIN_COLS // 2, 0, GRID_W - NA_WIN_COLS)
    in_win = (kc >= cs) & (kc < cs + NA_WIN_COLS)
    dcol = jnp.clip(kc - c + NA_WIN_COLS - 1, 0, 2 * NA_WIN_COLS - 2)
    n_drow = 2 * NA_WIN_ROWS - 1
    by_dcol = rpb.astype(jnp.float32).reshape(NA_HEADS * n_drow, -1).T
    t = jnp.where(in_win[..., None], _select_rows(by_dcol, dcol), NEG_INF)
    t = t.transpose(2, 0, 1).reshape(NA_HEADS, n_drow, GRID_W, GRID_W)
    t = jnp.concatenate([t, jnp.full_like(t[:, :1], NEG_INF)], axis=1)
    return jnp.concatenate([t, t], axis=-1)


def _wg_band_bias(t5):
    j = jnp.arange(3 * BLOCK)[:, None]
    i = jnp.arange(BLOCK)[None, :]
    rel = j - BLOCK - i
    tb = jnp.where((jnp.abs(rel) <= WINDOW)[..., None], _select_rows(t5, _t5_bucket(rel)),
                   NEG_INF)
    tb = tb.transpose(2, 0, 1).reshape(WG_KV_HEADS, WG_GROUP, 3 * BLOCK, BLOCK)
    return tb.transpose(0, 2, 1, 3).reshape(WG_KV_HEADS, 3 * BLOCK, WG_GROUP * BLOCK)


def _wg_meta_bias(t5):
    assert BLOCK + 1 >= T5_MAX_DIST
    rel = jnp.arange(N_META)[None, :] - (N_META + jnp.arange(2 * BLOCK)[:, None])
    bm = _select_rows(t5, _t5_bucket(rel))
    bm = bm.reshape(2, BLOCK, N_META, WG_KV_HEADS, WG_GROUP)
    return bm.transpose(3, 0, 2, 4, 1).reshape(WG_KV_HEADS, 2, N_META, WG_GROUP * BLOCK)


def _encode(x, meta_proj, w_in_t, g1, gcol, na_slabs, ga_col, band, bmeta, sink_row, gb_col,
            wo_a, wo_b, g2, w_gate, w_up, w_down):
    proj_t = _in_proj(x, g1, w_in_t, gcol, PROJ_TM)
    mix_a = _na_attn(proj_t, meta_proj[KA0:KA0 + NA_WIDTH], meta_proj[VA0:VA0 + NA_WIDTH],
                     na_slabs, ga_col)
    mix_b = _wg_attn(proj_t, meta_proj[KB0:KB0 + WG_KV_WIDTH], meta_proj[VB0:VB0 + WG_KV_WIDTH],
                     band, bmeta, sink_row, gb_col)
    return _out_ffn(mix_a, mix_b, x, wo_a, wo_b, g2, w_gate, w_up, w_down, FFN_TM)


def kernel(x_prompt, x_sample, meta_tokens, t5_table, norm1_g, w_in, qn_a_g, kn_a_g, rpb_a,
           qn_b_g, kn_b_g, sink_b, outn_a_g, outn_b_g, w_out, norm2_g, w_gate, w_up, w_down):
    f32, bf16 = jnp.float32, jnp.bfloat16
    scale = HEAD_DIM ** -0.5
    w_in_t = w_in[0].T.astype(bf16)
    g1 = norm1_g[0].astype(f32).reshape(1, D_MODEL)
    g2 = norm2_g[0].astype(f32).reshape(1, D_MODEL)
    ones = jnp.ones
    gcol = jnp.concatenate([
        jnp.tile(qn_a_g[0].astype(f32), NA_HEADS) * scale,
        jnp.tile(kn_a_g[0].astype(f32), NA_HEADS),
        ones((NA_WIDTH,), f32),
        jnp.tile(qn_b_g[0].astype(f32), WG_HEADS) * scale,
        jnp.tile(kn_b_g[0].astype(f32), WG_KV_HEADS),
        ones((WG_KV_WIDTH,), f32),
    ]).reshape(IN_WIDTH, 1)
    ga_col = outn_a_g[0].astype(f32).reshape(NA_WIDTH, 1)
    gb_col = outn_b_g[0].astype(f32).reshape(WG_WIDTH, 1)
    t5 = t5_table.astype(f32)
    na_slabs = _na_bias_slabs(rpb_a[0])
    band = _wg_band_bias(t5)
    bmeta = _wg_meta_bias(t5)
    sink_row = jnp.repeat(sink_b[0].astype(f32).reshape(WG_KV_HEADS, WG_GROUP), BLOCK,
                          axis=1).reshape(WG_KV_HEADS, 1, WG_GROUP * BLOCK)
    wo = w_out[0].astype(bf16)
    wo_a, wo_b = wo[:NA_WIDTH], wo[NA_WIDTH:]
    wg, wu, wd = w_gate[0].astype(bf16), w_up[0].astype(bf16), w_down[0].astype(bf16)

    meta = jnp.zeros((1, META_PAD, D_MODEL), f32).at[0, :N_META].set(meta_tokens.astype(f32))
    meta_proj = _in_proj(meta, g1, w_in_t, gcol, META_PAD)[0, :, :N_META]

    def enc(x):
        return _encode(x, meta_proj, w_in_t, g1, gcol, na_slabs, ga_col, band, bmeta, sink_row,
                       gb_col, wo_a, wo_b, g2, wg, wu, wd)

    return (enc(x_prompt), enc(x_sample))
```

```python
import functools
import math

import jax
import jax.numpy as jnp
import numpy as np
from jax import lax
from jax.experimental import pallas as pl
from jax.experimental.pallas import tpu as pltpu

D_MODEL = 1024
HEAD_DIM = 64
NA_HEADS = 8
WG_HEADS = 8
WG_KV_HEADS = 2
WG_GROUP = WG_HEADS // WG_KV_HEADS
NA_WIDTH = NA_HEADS * HEAD_DIM
WG_WIDTH = WG_HEADS * HEAD_DIM
WG_KV_WIDTH = WG_KV_HEADS * HEAD_DIM
IN_WIDTH = 3 * NA_WIDTH + WG_WIDTH + 2 * WG_KV_WIDTH
D_FF = 2816
GRID_W = 64
NA_WIN_ROWS = 8
NA_WIN_COLS = 16
N_META = 16
WINDOW = 128
BLOCK = 128
T5_BUCKETS = 32
T5_MAX_DIST = 128
EPS = 1e-6
NEG_INF = -1e30

QA0, KA0, VA0 = 0, NA_WIDTH, 2 * NA_WIDTH
QB0 = 3 * NA_WIDTH
KB0 = QB0 + WG_WIDTH
VB0 = KB0 + WG_KV_WIDTH

NA_UNIT_ROWS = 4
NA_UNIT = NA_UNIT_ROWS * GRID_W
NA_KEY_UNITS = 3
NA_NEG_SLAB = 2 * NA_WIN_ROWS - 1
NA_HALF = GRID_W // 2
NA_KSPAN = NA_HALF + NA_WIN_COLS // 2
NA_KOFF = GRID_W - NA_KSPAN
WG_QB = 4

PROJ_TM = 512
FFN_TM = 256
META_PAD = 128

VMEM_LIMIT_BYTES = 56 * 1024 * 1024

_TN = (((0,), (0,)), ((), ()))
_NT = (((1,), (1,)), ((), ()))
_NN = (((1,), (0,)), ((), ()))


def _params(n_grid_dims):
    return pltpu.CompilerParams(
        dimension_semantics=("arbitrary",) * n_grid_dims,
        vmem_limit_bytes=VMEM_LIMIT_BYTES)


_PROJ_CHUNKS = (
    (QA0, NA_WIDTH, True, True),
    (KA0, NA_WIDTH, True, False),
    (VA0, NA_WIDTH, False, False),
    (QB0, WG_WIDTH, True, False),
    (KB0, WG_KV_WIDTH, True, False),
    (VB0, WG_KV_WIDTH, False, False),
)


def _na_query_order(a, inverse=False):
    chunks = []
    for u in range(a.shape[0] // NA_UNIT):
        for outer in range(NA_UNIT_ROWS if inverse else 2):
            for inner in range(2 if inverse else NA_UNIT_ROWS):
                rr, half = (outer, inner) if inverse else (inner, outer)
                src = (half * NA_UNIT_ROWS + rr) if inverse else (rr * 2 + half)
                r0 = u * NA_UNIT + src * NA_HALF
                chunks.append(a[r0:r0 + NA_HALF])
    return jnp.concatenate(chunks, axis=0)


def _in_proj_kernel(x_ref, g1_ref, w_ref, gcol_ref, o_ref):
    x = x_ref[0]
    ms = jnp.mean(x * x, axis=-1, keepdims=True)
    h = (x * lax.rsqrt(ms + EPS) * g1_ref[...]).astype(jnp.bfloat16)
    h_na = _na_query_order(h) if h.shape[0] % NA_UNIT == 0 else h
    for row0, rows, normed, na_order in _PROJ_CHUNKS:
        p = lax.dot_general(w_ref[row0:row0 + rows, :], h_na if na_order else h, _NT,
                            preferred_element_type=jnp.float32)
        if not normed:
            o_ref[0, row0:row0 + rows, :] = p.astype(jnp.bfloat16)
            continue
        for i in range(rows // HEAD_DIM):
            blk = p[i * HEAD_DIM:(i + 1) * HEAD_DIM, :]
            ss = jnp.mean(blk * blk, axis=0, keepdims=True)
            r0 = row0 + i * HEAD_DIM
            y = blk * lax.rsqrt(ss + EPS) * gcol_ref[r0:r0 + HEAD_DIM, :]
            o_ref[0, r0:r0 + HEAD_DIM, :] = y.astype(jnp.bfloat16)


def _in_proj(x, g1, w_t, gcol, tm):
    b, n, _ = x.shape
    return pl.pallas_call(
        _in_proj_kernel,
        grid=(b, n // tm),
        in_specs=[
            pl.BlockSpec((1, tm, D_MODEL), lambda i, j: (i, j, 0)),
            pl.BlockSpec((1, D_MODEL), lambda i, j: (0, 0)),
            pl.BlockSpec((IN_WIDTH, D_MODEL), lambda i, j: (0, 0)),
            pl.BlockSpec((IN_WIDTH, 1), lambda i, j: (0, 0)),
        ],
        out_specs=pl.BlockSpec((1, IN_WIDTH, tm), lambda i, j: (i, 0, j)),
        out_shape=jax.ShapeDtypeStruct((b, IN_WIDTH, n), jnp.bfloat16),
        compiler_params=_params(2),
        name="in_proj",
    )(x, g1, w_t, gcol)


def _na_kernel(q_ref, k0_ref, k1_ref, k2_ref, v0_ref, v1_ref, v2_ref,
               km_ref, vm_ref, t_ref, g_ref, o_ref, acc_ref):
    k_refs = (k0_ref, k1_ref, k2_ref)
    v_refs = (v0_ref, v1_ref, v2_ref)
    lanes = (slice(0, 2 * GRID_W), slice(2 * GRID_W, 4 * GRID_W))

    def scores(h):
        hs = slice(h * HEAD_DIM, (h + 1) * HEAD_DIM)
        q = q_ref[0, hs, :]
        blocks = []
        for c in range(NA_KEY_UNITS):
            s = lax.dot_general(k_refs[c][0, hs, :], q, _TN,
                                preferred_element_type=jnp.float32)
            for jj in range(NA_UNIT_ROWS):
                j = NA_UNIT_ROWS * c + jj
                pair = []
                for half in range(2):
                    r0 = jj * GRID_W + half * NA_KOFF
                    pair.append(s[r0:r0 + NA_KSPAN, lanes[half]] + t_ref[0, h, j, half])
                blocks.append(pair)
        s_meta = lax.dot_general(km_ref[hs, :], q, _TN,
                                 preferred_element_type=jnp.float32)
        return blocks, s_meta

    def finish(h, blocks, s_meta):
        hs = slice(h * HEAD_DIM, (h + 1) * HEAD_DIM)
        m = jnp.max(s_meta, axis=0, keepdims=True)
        m = [m[:, lanes[0]], m[:, lanes[1]]]
        for pair in blocks:
            for half in range(2):
                m[half] = jnp.maximum(m[half], jnp.max(pair[half], axis=0, keepdims=True))
        p_meta = jnp.exp(s_meta - jnp.concatenate(m, axis=1))
        l = jnp.sum(p_meta, axis=0, keepdims=True)
        l = [l[:, lanes[0]], l[:, lanes[1]]]
        o = lax.dot_general(vm_ref[hs, :], p_meta.astype(jnp.bfloat16), _NN,
                            preferred_element_type=jnp.float32)
        pad = jnp.zeros((NA_KOFF, 2 * GRID_W), jnp.float32)
        for c in range(NA_KEY_UNITS):
            p_rows = []
            for jj in range(NA_UNIT_ROWS):
                pair = blocks[NA_UNIT_ROWS * c + jj]
                p = [jnp.exp(pair[half] - m[half]) for half in range(2)]
                for half in range(2):
                    l[half] = l[half] + jnp.sum(p[half], axis=0, keepdims=True)
                p_rows.append(jnp.concatenate(
                    [jnp.concatenate([p[0], pad], axis=0),
                     jnp.concatenate([pad, p[1]], axis=0)], axis=1))
            p_c = jnp.concatenate(p_rows, axis=0).astype(jnp.bfloat16)
            o = o + lax.dot_general(v_refs[c][0, hs, :], p_c, _NN,
                                    preferred_element_type=jnp.float32)
        acc_ref[hs, :] = o * (1.0 / jnp.concatenate(l, axis=1))

    nxt = scores(0)
    for h in range(NA_HEADS):
        cur = nxt
        if h + 1 < NA_HEADS:
            nxt = scores(h + 1)
        finish(h, *cur)

    full = acc_ref[...]
    ms = jnp.mean(full * full, axis=0, keepdims=True)
    o_ref[0] = (full * lax.rsqrt(ms + EPS) * g_ref[...]).astype(jnp.bfloat16)


def _na_slab_index(rows, u):
    n_units = rows // NA_UNIT_ROWS
    k_row0 = NA_UNIT_ROWS * min(max(u - 1, 0), n_units - NA_KEY_UNITS)
    idx = np.full((NA_KEY_UNITS * NA_UNIT_ROWS, NA_UNIT_ROWS), NA_NEG_SLAB, np.int32)
    for j in range(idx.shape[0]):
        for rr in range(NA_UNIT_ROWS):
            kr, r = k_row0 + j, NA_UNIT_ROWS * u + rr
            rs = min(max(r - NA_WIN_ROWS // 2, 0), rows - NA_WIN_ROWS)
            if rs <= kr < rs + NA_WIN_ROWS:
                idx[j, rr] = kr - r + NA_WIN_ROWS - 1
    return idx


def _na_variant(u, n_units):
    return jnp.where(u == 0, 0, jnp.where(u == n_units - 1, 2, 1))


def _na_variants(rows):
    n_units = rows // NA_UNIT_ROWS
    assert rows % NA_UNIT_ROWS == 0 and n_units >= NA_KEY_UNITS
    variants = np.stack([_na_slab_index(rows, u) for u in (0, 1, n_units - 1)])
    for u in range(n_units):
        v = 0 if u == 0 else (2 if u == n_units - 1 else 1)
        assert (_na_slab_index(rows, u) == variants[v]).all()
    return variants


def _na_attn(proj_t, km_t, vm_t, bias, variants, g_col):
    b, _, n = proj_t.shape
    n_units = n // NA_UNIT
    assert (_na_variants(n // GRID_W) == variants).all()

    def q_map(i, u):
        return (i, QA0 // NA_WIDTH, u)

    def kv_map(row_block, c):
        def index_map(i, u):
            return (i, row_block, jnp.clip(u - 1, 0, n_units - NA_KEY_UNITS) + c)
        return index_map

    blk = (1, NA_WIDTH, NA_UNIT)
    in_specs = [pl.BlockSpec(blk, q_map)]
    in_specs += [pl.BlockSpec(blk, kv_map(KA0 // NA_WIDTH, c)) for c in range(NA_KEY_UNITS)]
    in_specs += [pl.BlockSpec(blk, kv_map(VA0 // NA_WIDTH, c)) for c in range(NA_KEY_UNITS)]
    in_specs += [
        pl.BlockSpec((NA_WIDTH, N_META), lambda i, u: (0, 0)),
        pl.BlockSpec((NA_WIDTH, N_META), lambda i, u: (0, 0)),
        pl.BlockSpec((1,) + bias.shape[1:],
                     lambda i, u: (_na_variant(u, n_units), 0, 0, 0, 0, 0)),
        pl.BlockSpec((NA_WIDTH, 1), lambda i, u: (0, 0)),
    ]
    return pl.pallas_call(
        _na_kernel,
        grid=(b, n_units),
        in_specs=in_specs,
        out_specs=pl.BlockSpec(blk, lambda i, u: (i, 0, u)),
        out_shape=jax.ShapeDtypeStruct((b, NA_WIDTH, n), jnp.bfloat16),
        scratch_shapes=[pltpu.VMEM((NA_WIDTH, NA_UNIT), jnp.float32)],
        compiler_params=_params(2),
        name="na_attn",
    )(proj_t, proj_t, proj_t, proj_t, proj_t, proj_t, proj_t,
      km_t, vm_t, bias, g_col)


def _wg_kernel(q_ref, kl_ref, km_ref, kr_ref, vl_ref, vm_ref, vr_ref,
               kmeta_ref, vmeta_ref, band_ref, bmeta_ref, sink_ref, g_ref,
               o_ref, acc_ref, *, n_steps):
    j = pl.program_id(1)
    first = j == 0
    last = j == n_steps - 1

    def piece(left_ref, main_ref, right_ref, gs, i):
        if i == 0:
            return left_ref[0, gs, :]
        if i == WG_QB + 1:
            return right_ref[0, gs, :]
        return main_ref[0, gs, (i - 1) * BLOCK:i * BLOCK]

    def scores(g, qb):
        gs = slice(g * HEAD_DIM, (g + 1) * HEAD_DIM)
        q = jnp.concatenate(
            [q_ref[0, (g * WG_GROUP + hh) * HEAD_DIM:(g * WG_GROUP + hh + 1) * HEAD_DIM,
                   qb * BLOCK:(qb + 1) * BLOCK]
             for hh in range(WG_GROUP)], axis=1)
        s_blocks = []
        for c in range(3):
            s = lax.dot_general(piece(kl_ref, km_ref, kr_ref, gs, qb + c), q, _TN,
                                preferred_element_type=jnp.float32)
            s = s + band_ref[g, c * BLOCK:(c + 1) * BLOCK, :]
            if c == 0 and qb == 0:
                s = s + jnp.where(first, NEG_INF, 0.0)
            if c == 2 and qb == WG_QB - 1:
                s = s + jnp.where(last, NEG_INF, 0.0)
            s_blocks.append(s)
        meta_slab = jnp.where(first, 0, 1) if qb == 0 else 1
        s_meta = lax.dot_general(kmeta_ref[gs, :], q, _TN,
                                 preferred_element_type=jnp.float32) + bmeta_ref[g, meta_slab]
        return s_blocks, s_meta

    def finish(g, qb, s_blocks, s_meta):
        gs = slice(g * HEAD_DIM, (g + 1) * HEAD_DIM)
        sink = sink_ref[g]
        m = jnp.maximum(sink, jnp.max(s_meta, axis=0, keepdims=True))
        for s in s_blocks:
            m = jnp.maximum(m, jnp.max(s, axis=0, keepdims=True))
        p_meta = jnp.exp(s_meta - m)
        l = jnp.exp(sink - m) + jnp.sum(p_meta, axis=0, keepdims=True)
        o = lax.dot_general(vmeta_ref[gs, :], p_meta.astype(jnp.bfloat16), _NN,
                            preferred_element_type=jnp.float32)
        for c in range(3):
            p = jnp.exp(s_blocks[c] - m)
            l = l + jnp.sum(p, axis=0, keepdims=True)
            o = o + lax.dot_general(piece(vl_ref, vm_ref, vr_ref, gs, qb + c),
                                    p.astype(jnp.bfloat16), _NN,
                                    preferred_element_type=jnp.float32)
        o = o * (1.0 / l)
        for hh in range(WG_GROUP):
            r0 = (g * WG_GROUP + hh) * HEAD_DIM
            acc_ref[r0:r0 + HEAD_DIM, qb * BLOCK:(qb + 1) * BLOCK] = (
                o[:, hh * BLOCK:(hh + 1) * BLOCK])

    units = [(g, qb) for qb in range(WG_QB) for g in range(WG_KV_HEADS)]
    nxt = scores(*units[0])
    for i, unit in enumerate(units):
        cur = nxt
        if i + 1 < len(units):
            nxt = scores(*units[i + 1])
        finish(*unit, *cur)

    full = acc_ref[...]
    ms = jnp.mean(full * full, axis=0, keepdims=True)
    o_ref[0] = (full * lax.rsqrt(ms + EPS) * g_ref[...]).astype(jnp.bfloat16)


def _wg_attn(proj_t, km_t, vm_t, band, bmeta, sink_row, g_col):
    b, _, n = proj_t.shape
    nblk = n // BLOCK
    n_steps = nblk // WG_QB
    assert nblk % WG_QB == 0

    def main_map(row_block):
        return lambda i, j: (i, row_block, j)

    def left_map(row_block):
        return lambda i, j: (i, row_block, jnp.maximum(j * WG_QB - 1, 0))

    def right_map(row_block):
        return lambda i, j: (i, row_block, jnp.minimum((j + 1) * WG_QB, nblk - 1))

    tq = WG_QB * BLOCK
    kb, vb = KB0 // WG_KV_WIDTH, VB0 // WG_KV_WIDTH
    side_blk = (1, WG_KV_WIDTH, BLOCK)
    main_blk = (1, WG_KV_WIDTH, tq)
    in_specs = [
        pl.BlockSpec((1, WG_WIDTH, tq), lambda i, j: (i, QB0 // WG_WIDTH, j)),
        pl.BlockSpec(side_blk, left_map(kb)), pl.BlockSpec(main_blk, main_map(kb)),
        pl.BlockSpec(side_blk, right_map(kb)),
        pl.BlockSpec(side_blk, left_map(vb)), pl.BlockSpec(main_blk, main_map(vb)),
        pl.BlockSpec(side_blk, right_map(vb)),
        pl.BlockSpec((WG_KV_WIDTH, N_META), lambda i, j: (0, 0)),
        pl.BlockSpec((WG_KV_WIDTH, N_META), lambda i, j: (0, 0)),
        pl.BlockSpec(band.shape, lambda i, j: (0, 0, 0)),
        pl.BlockSpec(bmeta.shape, lambda i, j: (0, 0, 0, 0)),
        pl.BlockSpec(sink_row.shape, lambda i, j: (0, 0, 0)),
        pl.BlockSpec((WG_WIDTH, 1), lambda i, j: (0, 0)),
    ]
    return pl.pallas_call(
        functools.partial(_wg_kernel, n_steps=n_steps),
        grid=(b, n_steps),
        in_specs=in_specs,
        out_specs=pl.BlockSpec((1, WG_WIDTH, tq), lambda i, j: (i, 0, j)),
        out_shape=jax.ShapeDtypeStruct((b, WG_WIDTH, n), jnp.bfloat16),
        scratch_shapes=[pltpu.VMEM((WG_WIDTH, tq), jnp.float32)],
        compiler_params=_params(2),
        name="wg_attn",
    )(proj_t, proj_t, proj_t, proj_t, proj_t, proj_t, proj_t,
      km_t, vm_t, band, bmeta, sink_row, g_col)


def _out_ffn_kernel(ma_ref, mb_ref, x_ref, woa_ref, wob_ref, g2_ref,
                    wg_ref, wu_ref, wd_ref, o_ref):
    attn = lax.dot_general(ma_ref[0], woa_ref[...], _TN,
                           preferred_element_type=jnp.float32)
    attn = _na_query_order(attn, inverse=True)
    attn = attn + lax.dot_general(mb_ref[0], wob_ref[...], _TN,
                                  preferred_element_type=jnp.float32)
    x1 = x_ref[0] + attn
    ms = jnp.mean(x1 * x1, axis=-1, keepdims=True)
    h = (x1 * lax.rsqrt(ms + EPS) * g2_ref[...]).astype(jnp.bfloat16)
    gate = jnp.dot(h, wg_ref[...], preferred_element_type=jnp.float32)
    up = jnp.dot(h, wu_ref[...], preferred_element_type=jnp.float32)
    act = (gate * jax.nn.sigmoid(gate) * up).astype(jnp.bfloat16)
    o_ref[0] = x1 + jnp.dot(act, wd_ref[...], preferred_element_type=jnp.float32)


def _out_ffn(mix_a, mix_b, x, wo_a, wo_b, g2, w_gate, w_up, w_down, tm):
    b, n, _ = x.shape

    def const(shape):
        return pl.BlockSpec(shape, lambda i, j: (0,) * len(shape),
                            pipeline_mode=pl.Buffered(1))

    mix_spec = pl.BlockSpec((1, NA_WIDTH, tm), lambda i, j: (i, 0, j))
    x_spec = pl.BlockSpec((1, tm, D_MODEL), lambda i, j: (i, j, 0))
    return pl.pallas_call(
        _out_ffn_kernel,
        grid=(b, n // tm),
        in_specs=[mix_spec, mix_spec, x_spec,
                  const((NA_WIDTH, D_MODEL)), const((WG_WIDTH, D_MODEL)),
                  const((1, D_MODEL)),
                  const((D_MODEL, D_FF)), const((D_MODEL, D_FF)),
                  const((D_FF, D_MODEL))],
        out_specs=x_spec,
        out_shape=jax.ShapeDtypeStruct((b, n, D_MODEL), jnp.float32),
        compiler_params=_params(2),
        name="out_ffn",
    )(mix_a, mix_b, x, wo_a, wo_b, g2, w_gate, w_up, w_down)


def _t5_bucket(rel):
    half = T5_BUCKETS // 2
    max_exact = half // 2
    ret = jnp.where(rel > 0, half, 0)
    n = jnp.abs(rel)
    nf = jnp.maximum(n, 1).astype(jnp.float32)
    large = max_exact + (jnp.log(nf / max_exact) / math.log(T5_MAX_DIST / max_exact)
                         * (half - max_exact)).astype(jnp.int32)
    large = jnp.minimum(large, half - 1)
    return ret + jnp.where(n < max_exact, n, large)


def _select_rows(table, idx):
    onehot = idx[..., None] == jnp.arange(table.shape[0])
    return jnp.sum(jnp.where(onehot[..., None], table, 0.0), axis=-2)


def _na_bias_table(rpb, variants):
    kc = jnp.arange(GRID_W)[:, None]
    c = jnp.arange(GRID_W)[None, :]
    cs = jnp.clip(c - NA_WIN_COLS // 2, 0, GRID_W - NA_WIN_COLS)
    in_win = (kc >= cs) & (kc < cs + NA_WIN_COLS)
    dcol = jnp.clip(kc - c + NA_WIN_COLS - 1, 0, 2 * NA_WIN_COLS - 2)
    n_drow = 2 * NA_WIN_ROWS - 1
    by_dcol = rpb.astype(jnp.float32).reshape(NA_HEADS * n_drow, -1).T
    t = jnp.where(in_win[..., None], _select_rows(by_dcol, dcol), NEG_INF)
    t = t.transpose(2, 0, 1).reshape(NA_HEADS, n_drow, GRID_W, GRID_W)
    t = jnp.concatenate([t, jnp.full_like(t[:, :1], NEG_INF)], axis=1)
    n_var, n_j, n_rr = variants.shape
    halves = []
    for half in range(2):
        sub = t[:, :, half * NA_KOFF:half * NA_KOFF + NA_KSPAN,
                half * NA_HALF:(half + 1) * NA_HALF]
        sub = jnp.take(sub, variants.reshape(-1), axis=1)
        sub = sub.reshape(NA_HEADS, n_var, n_j, n_rr, NA_KSPAN, NA_HALF)
        halves.append(sub.transpose(1, 0, 2, 4, 3, 5).reshape(
            n_var, NA_HEADS, n_j, NA_KSPAN, n_rr * NA_HALF))
    return jnp.stack(halves, axis=3)


def _wg_band_bias(t5):
    j = jnp.arange(3 * BLOCK)[:, None]
    i = jnp.arange(BLOCK)[None, :]
    rel = j - BLOCK - i
    tb = jnp.where((jnp.abs(rel) <= WINDOW)[..., None], _select_rows(t5, _t5_bucket(rel)),
                   NEG_INF)
    tb = tb.transpose(2, 0, 1).reshape(WG_KV_HEADS, WG_GROUP, 3 * BLOCK, BLOCK)
    return tb.transpose(0, 2, 1, 3).reshape(WG_KV_HEADS, 3 * BLOCK, WG_GROUP * BLOCK)


def _wg_meta_bias(t5):
    assert BLOCK + 1 >= T5_MAX_DIST
    rel = jnp.arange(N_META)[None, :] - (N_META + jnp.arange(2 * BLOCK)[:, None])
    bm = _select_rows(t5, _t5_bucket(rel))
    bm = bm.reshape(2, BLOCK, N_META, WG_KV_HEADS, WG_GROUP)
    return bm.transpose(3, 0, 2, 4, 1).reshape(WG_KV_HEADS, 2, N_META, WG_GROUP * BLOCK)


def _encode(x, meta_proj, w_in_t, g1, gcol, na_bias, na_variants, ga_col, band, bmeta, sink_row,
            gb_col, wo_a, wo_b, g2, w_gate, w_up, w_down):
    proj_t = _in_proj(x, g1, w_in_t, gcol, PROJ_TM)
    mix_a = _na_attn(proj_t, meta_proj[KA0:KA0 + NA_WIDTH], meta_proj[VA0:VA0 + NA_WIDTH],
                     na_bias, na_variants, ga_col)
    mix_b = _wg_attn(proj_t, meta_proj[KB0:KB0 + WG_KV_WIDTH], meta_proj[VB0:VB0 + WG_KV_WIDTH],
                     band, bmeta, sink_row, gb_col)
    return _out_ffn(mix_a, mix_b, x, wo_a, wo_b, g2, w_gate, w_up, w_down, FFN_TM)


def kernel(x_prompt, x_sample, meta_tokens, t5_table, norm1_g, w_in, qn_a_g, kn_a_g, rpb_a,
           qn_b_g, kn_b_g, sink_b, outn_a_g, outn_b_g, w_out, norm2_g, w_gate, w_up, w_down):
    f32, bf16 = jnp.float32, jnp.bfloat16
    scale = HEAD_DIM ** -0.5
    w_in_t = w_in[0].T.astype(bf16)
    g1 = norm1_g[0].astype(f32).reshape(1, D_MODEL)
    g2 = norm2_g[0].astype(f32).reshape(1, D_MODEL)
    ones = jnp.ones
    gcol = jnp.concatenate([
        jnp.tile(qn_a_g[0].astype(f32), NA_HEADS) * scale,
        jnp.tile(kn_a_g[0].astype(f32), NA_HEADS),
        ones((NA_WIDTH,), f32),
        jnp.tile(qn_b_g[0].astype(f32), WG_HEADS) * scale,
        jnp.tile(kn_b_g[0].astype(f32), WG_KV_HEADS),
        ones((WG_KV_WIDTH,), f32),
    ]).reshape(IN_WIDTH, 1)
    ga_col = outn_a_g[0].astype(f32).reshape(NA_WIDTH, 1)
    gb_col = outn_b_g[0].astype(f32).reshape(WG_WIDTH, 1)
    t5 = t5_table.astype(f32)
    na_variants = _na_variants(x_prompt.shape[1] // GRID_W)
    na_bias = _na_bias_table(rpb_a[0], na_variants)
    band = _wg_band_bias(t5)
    bmeta = _wg_meta_bias(t5)
    sink_row = jnp.repeat(sink_b[0].astype(f32).reshape(WG_KV_HEADS, WG_GROUP), BLOCK,
                          axis=1).reshape(WG_KV_HEADS, 1, WG_GROUP * BLOCK)
    wo = w_out[0].astype(bf16)
    wo_a, wo_b = wo[:NA_WIDTH], wo[NA_WIDTH:]
    wg, wu, wd = w_gate[0].astype(bf16), w_up[0].astype(bf16), w_down[0].astype(bf16)

    meta = jnp.zeros((1, META_PAD, D_MODEL), f32).at[0, :N_META].set(meta_tokens.astype(f32))
    meta_proj = _in_proj(meta, g1, w_in_t, gcol, META_PAD)[0, :, :N_META]

    def enc(x):
        return _encode(x, meta_proj, w_in_t, g1, gcol, na_bias, na_variants, ga_col, band, bmeta,
                       sink_row, gb_col, wo_a, wo_b, g2, wg, wu, wd)

    return (enc(x_prompt), enc(x_sample))
```

```python
import functools
import math

import jax
import jax.numpy as jnp
import numpy as np
from jax import lax
from jax.experimental import pallas as pl
from jax.experimental.pallas import tpu as pltpu

D_MODEL = 1024
HEAD_DIM = 64
NA_HEADS = 8
WG_HEADS = 8
WG_KV_HEADS = 2
WG_GROUP = WG_HEADS // WG_KV_HEADS
NA_WIDTH = NA_HEADS * HEAD_DIM
WG_WIDTH = WG_HEADS * HEAD_DIM
WG_KV_WIDTH = WG_KV_HEADS * HEAD_DIM
IN_WIDTH = 3 * NA_WIDTH + WG_WIDTH + 2 * WG_KV_WIDTH
D_FF = 2816
GRID_W = 64
NA_WIN_ROWS = 8
NA_WIN_COLS = 16
N_META = 16
WINDOW = 128
BLOCK = 128
T5_BUCKETS = 32
T5_MAX_DIST = 128
EPS = 1e-6
NEG_INF = -1e30

QA0 = 0
QB0 = QA0 + NA_WIDTH
KA0 = QB0 + WG_WIDTH
VA0 = KA0 + NA_WIDTH
KB0 = VA0 + NA_WIDTH
VB0 = KB0 + WG_KV_WIDTH
LOG2E = math.log2(math.e)
SPLITS_IN = (NA_WIDTH, 2 * NA_WIDTH, 3 * NA_WIDTH, 3 * NA_WIDTH + WG_WIDTH,
             3 * NA_WIDTH + WG_WIDTH + WG_KV_WIDTH)

NA_UNIT_ROWS = 4
NA_UNIT = NA_UNIT_ROWS * GRID_W
NA_KEY_UNITS = 3
NA_NEG_SLAB = 2 * NA_WIN_ROWS - 1
NA_HALF = GRID_W // 2
NA_KSPAN = NA_HALF + NA_WIN_COLS // 2
NA_KOFF = GRID_W - NA_KSPAN
WG_QB = 4

PROJ_TM = 512
FFN_TM = 256
META_PAD = 128

VMEM_LIMIT_BYTES = 56 * 1024 * 1024

_TN = (((0,), (0,)), ((), ()))
_NT = (((1,), (1,)), ((), ()))
_NN = (((1,), (0,)), ((), ()))


def _params(n_grid_dims):
    return pltpu.CompilerParams(
        dimension_semantics=("arbitrary",) * n_grid_dims,
        vmem_limit_bytes=VMEM_LIMIT_BYTES)


_PROJ_CHUNKS = (
    (QA0, NA_WIDTH, True, True),
    (QB0, WG_WIDTH, True, False),
    (KA0, NA_WIDTH, True, False),
    (VA0, NA_WIDTH, False, False),
    (KB0, WG_KV_WIDTH, True, False),
    (VB0, WG_KV_WIDTH, False, False),
)


def _na_query_order(a, inverse=False):
    chunks = []
    for u in range(a.shape[0] // NA_UNIT):
        for outer in range(NA_UNIT_ROWS if inverse else 2):
            for inner in range(2 if inverse else NA_UNIT_ROWS):
                rr, half = (outer, inner) if inverse else (inner, outer)
                src = (half * NA_UNIT_ROWS + rr) if inverse else (rr * 2 + half)
                r0 = u * NA_UNIT + src * NA_HALF
                chunks.append(a[r0:r0 + NA_HALF])
    return jnp.concatenate(chunks, axis=0)


def _in_proj_kernel(x_ref, g1_ref, w_ref, gcol_ref, o_ref):
    x = x_ref[0]
    ms = jnp.mean(x * x, axis=-1, keepdims=True)
    h = (x * lax.rsqrt(ms + EPS) * g1_ref[...]).astype(jnp.bfloat16)
    h_na = _na_query_order(h) if h.shape[0] % NA_UNIT == 0 else h
    for row0, rows, normed, na_order in _PROJ_CHUNKS:
        p = lax.dot_general(w_ref[row0:row0 + rows, :], h_na if na_order else h, _NT,
                            preferred_element_type=jnp.float32)
        if not normed:
            o_ref[0, row0:row0 + rows, :] = p.astype(jnp.bfloat16)
            continue
        for i in range(rows // HEAD_DIM):
            blk = p[i * HEAD_DIM:(i + 1) * HEAD_DIM, :]
            ss = jnp.mean(blk * blk, axis=0, keepdims=True)
            r0 = row0 + i * HEAD_DIM
            y = blk * lax.rsqrt(ss + EPS) * gcol_ref[r0:r0 + HEAD_DIM, :]
            o_ref[0, r0:r0 + HEAD_DIM, :] = y.astype(jnp.bfloat16)


def _in_proj(x, g1, w_t, gcol, tm):
    b, n, _ = x.shape
    return pl.pallas_call(
        _in_proj_kernel,
        grid=(b, n // tm),
        in_specs=[
            pl.BlockSpec((1, tm, D_MODEL), lambda i, j: (i, j, 0)),
            pl.BlockSpec((1, D_MODEL), lambda i, j: (0, 0)),
            pl.BlockSpec((IN_WIDTH, D_MODEL), lambda i, j: (0, 0)),
            pl.BlockSpec((IN_WIDTH, 1), lambda i, j: (0, 0)),
        ],
        out_specs=pl.BlockSpec((1, IN_WIDTH, tm), lambda i, j: (i, 0, j)),
        out_shape=jax.ShapeDtypeStruct((b, IN_WIDTH, n), jnp.bfloat16),
        compiler_params=_params(2),
        name="in_proj",
    )(x, g1, w_t, gcol)


def _fold8(x, op):
    acc = x[0:8]
    for r in range(8, x.shape[0], 8):
        acc = op(acc, x[r:r + 8])
    return acc


def _na_kernel(q_ref, kv0_ref, kv1_ref, kv2_ref, kvm_ref, t_ref, g_ref, o_ref, acc_ref):
    kv_refs = (kv0_ref, kv1_ref, kv2_ref)
    lanes = (slice(0, 2 * GRID_W), slice(2 * GRID_W, 4 * GRID_W))

    def scores(h):
        hs = slice(h * HEAD_DIM, (h + 1) * HEAD_DIM)
        q = q_ref[0, hs, :]
        blocks = []
        for c in range(NA_KEY_UNITS):
            s = lax.dot_general(kv_refs[c][0, hs, :], q, _TN,
                                preferred_element_type=jnp.float32)
            for jj in range(NA_UNIT_ROWS):
                j = NA_UNIT_ROWS * c + jj
                pair = []
                for half in range(2):
                    r0 = jj * GRID_W + half * NA_KOFF
                    pair.append(s[r0:r0 + NA_KSPAN, lanes[half]] + t_ref[0, h, j, half])
                blocks.append(pair)
        s_meta = lax.dot_general(kvm_ref[hs, :], q, _TN,
                                 preferred_element_type=jnp.float32)
        return blocks, s_meta

    def finish(h, blocks, s_meta):
        hs = slice(h * HEAD_DIM, (h + 1) * HEAD_DIM)
        vs = slice(NA_WIDTH + h * HEAD_DIM, NA_WIDTH + (h + 1) * HEAD_DIM)
        m8 = _fold8(s_meta, jnp.maximum)
        m8 = [m8[:, lanes[0]], m8[:, lanes[1]]]
        for pair in blocks:
            for half in range(2):
                m8[half] = jnp.maximum(m8[half], _fold8(pair[half], jnp.maximum))
        m = [jnp.max(m8[half], axis=0, keepdims=True) for half in range(2)]
        p_meta = jnp.exp2(s_meta - jnp.concatenate(m, axis=1))
        l8 = _fold8(p_meta, jnp.add)
        l8 = [l8[:, lanes[0]], l8[:, lanes[1]]]
        o = lax.dot_general(kvm_ref[vs, :], p_meta.astype(jnp.bfloat16), _NN,
                            preferred_element_type=jnp.float32)
        pad = jnp.zeros((NA_KOFF, 2 * GRID_W), jnp.float32)
        for c in range(NA_KEY_UNITS):
            p_rows = []
            for jj in range(NA_UNIT_ROWS):
                pair = blocks[NA_UNIT_ROWS * c + jj]
                p = [jnp.exp2(pair[half] - m[half]) for half in range(2)]
                for half in range(2):
                    l8[half] = l8[half] + _fold8(p[half], jnp.add)
                p_rows.append(jnp.concatenate(
                    [jnp.concatenate([p[0], pad], axis=0),
                     jnp.concatenate([pad, p[1]], axis=0)], axis=1))
            p_c = jnp.concatenate(p_rows, axis=0).astype(jnp.bfloat16)
            o = o + lax.dot_general(kv_refs[c][0, vs, :], p_c, _NN,
                                    preferred_element_type=jnp.float32)
        l = jnp.concatenate([jnp.sum(l8[half], axis=0, keepdims=True) for half in range(2)],
                            axis=1)
        acc_ref[hs, :] = o * (1.0 / l)

    nxt = scores(0)
    for h in range(NA_HEADS):
        cur = nxt
        if h + 1 < NA_HEADS:
            nxt = scores(h + 1)
        finish(h, *cur)

    full = acc_ref[...]
    ms = jnp.mean(full * full, axis=0, keepdims=True)
    o_ref[0] = (full * lax.rsqrt(ms + EPS) * g_ref[...]).astype(jnp.bfloat16)


def _na_slab_index(rows, u):
    n_units = rows // NA_UNIT_ROWS
    k_row0 = NA_UNIT_ROWS * min(max(u - 1, 0), n_units - NA_KEY_UNITS)
    idx = np.full((NA_KEY_UNITS * NA_UNIT_ROWS, NA_UNIT_ROWS), NA_NEG_SLAB, np.int32)
    for j in range(idx.shape[0]):
        for rr in range(NA_UNIT_ROWS):
            kr, r = k_row0 + j, NA_UNIT_ROWS * u + rr
            rs = min(max(r - NA_WIN_ROWS // 2, 0), rows - NA_WIN_ROWS)
            if rs <= kr < rs + NA_WIN_ROWS:
                idx[j, rr] = kr - r + NA_WIN_ROWS - 1
    return idx


def _na_variant(u, n_units):
    return jnp.where(u == 0, 0, jnp.where(u == n_units - 1, 2, 1))


def _na_variants(rows):
    n_units = rows // NA_UNIT_ROWS
    assert rows % NA_UNIT_ROWS == 0 and n_units >= NA_KEY_UNITS
    variants = np.stack([_na_slab_index(rows, u) for u in (0, 1, n_units - 1)])
    for u in range(n_units):
        v = 0 if u == 0 else (2 if u == n_units - 1 else 1)
        assert (_na_slab_index(rows, u) == variants[v]).all()
    return variants


def _na_attn(proj_t, kvm_t, bias, variants, g_col):
    b, _, n = proj_t.shape
    n_units = n // NA_UNIT
    assert (_na_variants(n // GRID_W) == variants).all()

    def q_map(i, u):
        return (i, QA0 // NA_WIDTH, u)

    def kv_map(c):
        def index_map(i, u):
            return (i, KA0 // (2 * NA_WIDTH), jnp.clip(u - 1, 0, n_units - NA_KEY_UNITS) + c)
        return index_map

    assert VA0 == KA0 + NA_WIDTH and KA0 % (2 * NA_WIDTH) == 0
    blk = (1, NA_WIDTH, NA_UNIT)
    in_specs = [pl.BlockSpec(blk, q_map)]
    in_specs += [pl.BlockSpec((1, 2 * NA_WIDTH, NA_UNIT), kv_map(c)) for c in range(NA_KEY_UNITS)]
    in_specs += [
        pl.BlockSpec((2 * NA_WIDTH, N_META), lambda i, u: (0, 0)),
        pl.BlockSpec((1,) + bias.shape[1:],
                     lambda i, u: (_na_variant(u, n_units), 0, 0, 0, 0, 0)),
        pl.BlockSpec((NA_WIDTH, 1), lambda i, u: (0, 0)),
    ]
    return pl.pallas_call(
        _na_kernel,
        grid=(b, n_units),
        in_specs=in_specs,
        out_specs=pl.BlockSpec(blk, lambda i, u: (i, 0, u)),
        out_shape=jax.ShapeDtypeStruct((b, NA_WIDTH, n), jnp.bfloat16),
        scratch_shapes=[pltpu.VMEM((NA_WIDTH, NA_UNIT), jnp.float32)],
        compiler_params=_params(2),
        name="na_attn",
    )(proj_t, proj_t, proj_t, proj_t, kvm_t, bias, g_col)


def _wg_kernel(q_ref, kvl_ref, kvm_ref, kvr_ref, kvmeta_ref, band_ref, bmeta_ref,
               sink_ref, g_ref, o_ref, acc_ref, *, n_steps):
    j = pl.program_id(1)
    first = j == 0
    last = j == n_steps - 1

    def piece(rows, i):
        if i == 0:
            return kvl_ref[0, rows, :]
        if i == WG_QB + 1:
            return kvr_ref[0, rows, :]
        return kvm_ref[0, rows, (i - 1) * BLOCK:i * BLOCK]

    def scores(g, qb):
        gs = slice(g * HEAD_DIM, (g + 1) * HEAD_DIM)
        q = jnp.concatenate(
            [q_ref[0, (g * WG_GROUP + hh) * HEAD_DIM:(g * WG_GROUP + hh + 1) * HEAD_DIM,
                   qb * BLOCK:(qb + 1) * BLOCK]
             for hh in range(WG_GROUP)], axis=1)
        s_blocks = []
        for c in range(3):
            s = lax.dot_general(piece(gs, qb + c), q, _TN,
                                preferred_element_type=jnp.float32)
            s = s + band_ref[g, c * BLOCK:(c + 1) * BLOCK, :]
            if c == 0 and qb == 0:
                s = s + jnp.where(first, NEG_INF, 0.0)
            if c == 2 and qb == WG_QB - 1:
                s = s + jnp.where(last, NEG_INF, 0.0)
            s_blocks.append(s)
        meta_slab = jnp.where(first, 0, 1) if qb == 0 else 1
        s_meta = lax.dot_general(kvmeta_ref[gs, :], q, _TN,
                                 preferred_element_type=jnp.float32) + bmeta_ref[g, meta_slab]
        return s_blocks, s_meta

    def finish(g, qb, s_blocks, s_meta):
        vs = slice(WG_KV_WIDTH + g * HEAD_DIM, WG_KV_WIDTH + (g + 1) * HEAD_DIM)
        sink = sink_ref[g]
        m8 = _fold8(s_meta, jnp.maximum)
        for s in s_blocks:
            m8 = jnp.maximum(m8, _fold8(s, jnp.maximum))
        m = jnp.maximum(sink, jnp.max(m8, axis=0, keepdims=True))
        p_meta = jnp.exp2(s_meta - m)
        l8 = _fold8(p_meta, jnp.add)
        o = lax.dot_general(kvmeta_ref[vs, :], p_meta.astype(jnp.bfloat16), _NN,
                            preferred_element_type=jnp.float32)
        for c in range(3):
            p = jnp.exp2(s_blocks[c] - m)
            l8 = l8 + _fold8(p, jnp.add)
            o = o + lax.dot_general(piece(vs, qb + c), p.astype(jnp.bfloat16), _NN,
                                    preferred_element_type=jnp.float32)
        l = jnp.exp2(sink - m) + jnp.sum(l8, axis=0, keepdims=True)
        o = o * (1.0 / l)
        for hh in range(WG_GROUP):
            r0 = (g * WG_GROUP + hh) * HEAD_DIM
            acc_ref[r0:r0 + HEAD_DIM, qb * BLOCK:(qb + 1) * BLOCK] = (
                o[:, hh * BLOCK:(hh + 1) * BLOCK])

    units = [(g, qb) for qb in range(WG_QB) for g in range(WG_KV_HEADS)]
    nxt = scores(*units[0])
    for i, unit in enumerate(units):
        cur = nxt
        if i + 1 < len(units):
            nxt = scores(*units[i + 1])
        finish(*unit, *cur)

    full = acc_ref[...]
    ms = jnp.mean(full * full, axis=0, keepdims=True)
    o_ref[0] = (full * lax.rsqrt(ms + EPS) * g_ref[...]).astype(jnp.bfloat16)


def _wg_attn(proj_t, kvm_t, band, bmeta, sink_row, g_col):
    b, _, n = proj_t.shape
    nblk = n // BLOCK
    n_steps = nblk // WG_QB
    assert nblk % WG_QB == 0
    assert VB0 == KB0 + WG_KV_WIDTH and KB0 % (2 * WG_KV_WIDTH) == 0
    kv_rows = KB0 // (2 * WG_KV_WIDTH)

    tq = WG_QB * BLOCK
    side_blk = (1, 2 * WG_KV_WIDTH, BLOCK)
    in_specs = [
        pl.BlockSpec((1, WG_WIDTH, tq), lambda i, j: (i, QB0 // WG_WIDTH, j)),
        pl.BlockSpec(side_blk, lambda i, j: (i, kv_rows, jnp.maximum(j * WG_QB - 1, 0))),
        pl.BlockSpec((1, 2 * WG_KV_WIDTH, tq), lambda i, j: (i, kv_rows, j)),
        pl.BlockSpec(side_blk,
                     lambda i, j: (i, kv_rows, jnp.minimum((j + 1) * WG_QB, nblk - 1))),
        pl.BlockSpec((2 * WG_KV_WIDTH, N_META), lambda i, j: (0, 0)),
        pl.BlockSpec(band.shape, lambda i, j: (0, 0, 0)),
        pl.BlockSpec(bmeta.shape, lambda i, j: (0, 0, 0, 0)),
        pl.BlockSpec(sink_row.shape, lambda i, j: (0, 0, 0)),
        pl.BlockSpec((WG_WIDTH, 1), lambda i, j: (0, 0)),
    ]
    return pl.pallas_call(
        functools.partial(_wg_kernel, n_steps=n_steps),
        grid=(b, n_steps),
        in_specs=in_specs,
        out_specs=pl.BlockSpec((1, WG_WIDTH, tq), lambda i, j: (i, 0, j)),
        out_shape=jax.ShapeDtypeStruct((b, WG_WIDTH, n), jnp.bfloat16),
        scratch_shapes=[pltpu.VMEM((WG_WIDTH, tq), jnp.float32)],
        compiler_params=_params(2),
        name="wg_attn",
    )(proj_t, proj_t, proj_t, proj_t, kvm_t, band, bmeta, sink_row, g_col)


def _out_ffn_kernel(ma_ref, mb_ref, x_ref, woa_ref, wob_ref, g2_ref,
                    wg_ref, wu_ref, wd_ref, o_ref):
    attn = lax.dot_general(ma_ref[0], woa_ref[...], _TN,
                           preferred_element_type=jnp.float32)
    attn = _na_query_order(attn, inverse=True)
    attn = attn + lax.dot_general(mb_ref[0], wob_ref[...], _TN,
                                  preferred_element_type=jnp.float32)
    x1 = x_ref[0] + attn
    ms = jnp.mean(x1 * x1, axis=-1, keepdims=True)
    h = (x1 * lax.rsqrt(ms + EPS) * g2_ref[...]).astype(jnp.bfloat16)
    gate = jnp.dot(h, wg_ref[...], preferred_element_type=jnp.float32)
    up = jnp.dot(h, wu_ref[...], preferred_element_type=jnp.float32)
    act = (gate * jax.nn.sigmoid(gate) * up).astype(jnp.bfloat16)
    o_ref[0] = x1 + jnp.dot(act, wd_ref[...], preferred_element_type=jnp.float32)


def _out_ffn(mix_a, mix_b, x, wo_a, wo_b, g2, w_gate, w_up, w_down, tm):
    b, n, _ = x.shape

    def const(shape):
        return pl.BlockSpec(shape, lambda i, j: (0,) * len(shape),
                            pipeline_mode=pl.Buffered(1))

    mix_spec = pl.BlockSpec((1, NA_WIDTH, tm), lambda i, j: (i, 0, j))
    x_spec = pl.BlockSpec((1, tm, D_MODEL), lambda i, j: (i, j, 0))
    return pl.pallas_call(
        _out_ffn_kernel,
        grid=(b, n // tm),
        in_specs=[mix_spec, mix_spec, x_spec,
                  const((NA_WIDTH, D_MODEL)), const((WG_WIDTH, D_MODEL)),
                  const((1, D_MODEL)),
                  const((D_MODEL, D_FF)), const((D_MODEL, D_FF)),
                  const((D_FF, D_MODEL))],
        out_specs=x_spec,
        out_shape=jax.ShapeDtypeStruct((b, n, D_MODEL), jnp.float32),
        compiler_params=_params(2),
        name="out_ffn",
    )(mix_a, mix_b, x, wo_a, wo_b, g2, w_gate, w_up, w_down)


def _t5_bucket(rel):
    half = T5_BUCKETS // 2
    max_exact = half // 2
    ret = jnp.where(rel > 0, half, 0)
    n = jnp.abs(rel)
    nf = jnp.maximum(n, 1).astype(jnp.float32)
    large = max_exact + (jnp.log(nf / max_exact) / math.log(T5_MAX_DIST / max_exact)
                         * (half - max_exact)).astype(jnp.int32)
    large = jnp.minimum(large, half - 1)
    return ret + jnp.where(n < max_exact, n, large)


def _select_rows(table, idx):
    onehot = idx[..., None] == jnp.arange(table.shape[0])
    return jnp.sum(jnp.where(onehot[..., None], table, 0.0), axis=-2)


def _na_bias_table(rpb, variants):
    n_var, n_j, n_rr = variants.shape
    n_drow, n_dcol = 2 * NA_WIN_ROWS - 1, 2 * NA_WIN_COLS - 1
    sel_row = (variants[..., None] == np.arange(n_drow)).astype(np.float32)
    sel_col = np.zeros((2, NA_KSPAN, NA_HALF, n_dcol), np.float32)
    for half in range(2):
        for k in range(NA_KSPAN):
            for cc in range(NA_HALF):
                kc, c = half * NA_KOFF + k, half * NA_HALF + cc
                cs = min(max(c - NA_WIN_COLS // 2, 0), GRID_W - NA_WIN_COLS)
                if cs <= kc < cs + NA_WIN_COLS:
                    sel_col[half, k, cc, kc - c + NA_WIN_COLS - 1] = 1.0
    hi = lax.Precision.HIGHEST
    by_col = jnp.einsum("hde,akce->ahdkc", rpb.astype(jnp.float32), sel_col, precision=hi)
    table = jnp.einsum("vjrd,ahdkc->vhjakrc", sel_row, by_col, precision=hi)
    inside = (sel_row.sum(-1)[:, None, :, None, None, :, None]
              * sel_col.sum(-1)[None, None, None, :, :, None, :]) > 0
    table = jnp.where(inside, table, NEG_INF)
    return table.reshape(n_var, NA_HEADS, n_j, 2, NA_KSPAN, n_rr * NA_HALF)


def _wg_band_bias(t5):
    j = jnp.arange(3 * BLOCK)[:, None]
    i = jnp.arange(BLOCK)[None, :]
    rel = j - BLOCK - i
    tb = jnp.where((jnp.abs(rel) <= WINDOW)[..., None], _select_rows(t5, _t5_bucket(rel)),
                   NEG_INF)
    tb = tb.transpose(2, 0, 1).reshape(WG_KV_HEADS, WG_GROUP, 3 * BLOCK, BLOCK)
    return tb.transpose(0, 2, 1, 3).reshape(WG_KV_HEADS, 3 * BLOCK, WG_GROUP * BLOCK)


def _wg_meta_bias(t5):
    assert BLOCK + 1 >= T5_MAX_DIST
    rel = jnp.arange(N_META)[None, :] - (N_META + jnp.arange(2 * BLOCK)[:, None])
    bm = _select_rows(t5, _t5_bucket(rel))
    bm = bm.reshape(2, BLOCK, N_META, WG_KV_HEADS, WG_GROUP)
    return bm.transpose(3, 0, 2, 4, 1).reshape(WG_KV_HEADS, 2, N_META, WG_GROUP * BLOCK)


def _encode(x, meta_proj, w_in_t, g1, gcol, na_bias, na_variants, ga_col, band, bmeta, sink_row,
            gb_col, wo_a, wo_b, g2, w_gate, w_up, w_down):
    proj_t = _in_proj(x, g1, w_in_t, gcol, PROJ_TM)
    mix_a = _na_attn(proj_t, meta_proj[KA0:KA0 + 2 * NA_WIDTH], na_bias, na_variants, ga_col)
    mix_b = _wg_attn(proj_t, meta_proj[KB0:KB0 + 2 * WG_KV_WIDTH], band, bmeta, sink_row, gb_col)
    return _out_ffn(mix_a, mix_b, x, wo_a, wo_b, g2, w_gate, w_up, w_down, FFN_TM)


def kernel(x_prompt, x_sample, meta_tokens, t5_table, norm1_g, w_in, qn_a_g, kn_a_g, rpb_a,
           qn_b_g, kn_b_g, sink_b, outn_a_g, outn_b_g, w_out, norm2_g, w_gate, w_up, w_down):
    f32, bf16 = jnp.float32, jnp.bfloat16
    q_scale = HEAD_DIM ** -0.5 * LOG2E
    qa_w, ka_w, va_w, qb_w, kb_w, vb_w = jnp.split(w_in[0], SPLITS_IN, axis=1)
    w_in_t = jnp.concatenate([qa_w, qb_w, ka_w, va_w, kb_w, vb_w], axis=1).T.astype(bf16)
    g1 = norm1_g[0].astype(f32).reshape(1, D_MODEL)
    g2 = norm2_g[0].astype(f32).reshape(1, D_MODEL)
    ones = jnp.ones
    gcol = jnp.concatenate([
        jnp.tile(qn_a_g[0].astype(f32), NA_HEADS) * q_scale,
        jnp.tile(qn_b_g[0].astype(f32), WG_HEADS) * q_scale,
        jnp.tile(kn_a_g[0].astype(f32), NA_HEADS),
        ones((NA_WIDTH,), f32),
        jnp.tile(kn_b_g[0].astype(f32), WG_KV_HEADS),
        ones((WG_KV_WIDTH,), f32),
    ]).reshape(IN_WIDTH, 1)
    ga_col = outn_a_g[0].astype(f32).reshape(NA_WIDTH, 1)
    gb_col = outn_b_g[0].astype(f32).reshape(WG_WIDTH, 1)
    t5 = t5_table.astype(f32) * LOG2E
    na_variants = _na_variants(x_prompt.shape[1] // GRID_W)
    na_bias = _na_bias_table(rpb_a[0].astype(f32) * LOG2E, na_variants)
    band = _wg_band_bias(t5)
    bmeta = _wg_meta_bias(t5)
    sink_row = jnp.repeat(sink_b[0].astype(f32).reshape(WG_KV_HEADS, WG_GROUP) * LOG2E, BLOCK,
                          axis=1).reshape(WG_KV_HEADS, 1, WG_GROUP * BLOCK)
    wo = w_out[0].astype(bf16)
    wo_a, wo_b = wo[:NA_WIDTH], wo[NA_WIDTH:]
    wg, wu, wd = w_gate[0].astype(bf16), w_up[0].astype(bf16), w_down[0].astype(bf16)

    meta = jnp.zeros((1, META_PAD, D_MODEL), f32).at[0, :N_META].set(meta_tokens.astype(f32))
    meta_proj = _in_proj(meta, g1, w_in_t, gcol, META_PAD)[0, :, :N_META]

    def enc(x):
        return _encode(x, meta_proj, w_in_t, g1, gcol, na_bias, na_variants, ga_col, band, bmeta,
                       sink_row, gb_col, wo_a, wo_b, g2, wg, wu, wd)

    return (enc(x_prompt), enc(x_sample))
```

```python
import functools
import math

import jax
import jax.numpy as jnp
import numpy as np
from jax import lax
from jax.experimental import pallas as pl
from jax.experimental.pallas import tpu as pltpu

D_MODEL = 1024
HEAD_DIM = 64
NA_HEADS = 8
WG_HEADS = 8
WG_KV_HEADS = 2
WG_GROUP = WG_HEADS // WG_KV_HEADS
NA_WIDTH = NA_HEADS * HEAD_DIM
WG_WIDTH = WG_HEADS * HEAD_DIM
WG_KV_WIDTH = WG_KV_HEADS * HEAD_DIM
IN_WIDTH = 3 * NA_WIDTH + WG_WIDTH + 2 * WG_KV_WIDTH
D_FF = 2816
GRID_W = 64
NA_WIN_ROWS = 8
NA_WIN_COLS = 16
N_META = 16
WINDOW = 128
BLOCK = 128
T5_BUCKETS = 32
T5_MAX_DIST = 128
EPS = 1e-6
NEG_INF = -1e30

QA0 = 0
QB0 = QA0 + NA_WIDTH
KA0 = QB0 + WG_WIDTH
VA0 = KA0 + NA_WIDTH
KB0 = VA0 + NA_WIDTH
VB0 = KB0 + WG_KV_WIDTH
LOG2E = math.log2(math.e)
SPLITS_IN = (NA_WIDTH, 2 * NA_WIDTH, 3 * NA_WIDTH, 3 * NA_WIDTH + WG_WIDTH,
             3 * NA_WIDTH + WG_WIDTH + WG_KV_WIDTH)

NA_UNIT_ROWS = 4
NA_UNIT = NA_UNIT_ROWS * GRID_W
NA_KEY_UNITS = 3
NA_NEG_SLAB = 2 * NA_WIN_ROWS - 1
NA_HALF = GRID_W // 2
NA_KSPAN = NA_HALF + NA_WIN_COLS // 2
NA_KOFF = GRID_W - NA_KSPAN
WG_QB = 4

PROJ_TM = 1024
PROJ_TILE = 256
FFN_TM = 512
FFN_TILE = 256
FFN_FC = 256
FFN_LOOKAHEAD = 4
META_PAD = 128

VMEM_LIMIT_BYTES = 56 * 1024 * 1024

_TN = (((0,), (0,)), ((), ()))
_NT = (((1,), (1,)), ((), ()))
_NN = (((1,), (0,)), ((), ()))


def _params(n_grid_dims):
    return pltpu.CompilerParams(
        dimension_semantics=("arbitrary",) * n_grid_dims,
        vmem_limit_bytes=VMEM_LIMIT_BYTES)


_PROJ_CHUNKS = (
    (QA0, NA_WIDTH, NA_WIDTH, True),
    (QB0, WG_WIDTH, WG_WIDTH, False),
    (KA0, NA_WIDTH, NA_WIDTH, False),
    (VA0, NA_WIDTH, 0, False),
    (KB0, 2 * WG_KV_WIDTH, WG_KV_WIDTH, False),
)


def _na_query_order(a, inverse=False):
    chunks = []
    for u in range(a.shape[0] // NA_UNIT):
        for outer in range(NA_UNIT_ROWS if inverse else 2):
            for inner in range(2 if inverse else NA_UNIT_ROWS):
                rr, half = (outer, inner) if inverse else (inner, outer)
                src = (half * NA_UNIT_ROWS + rr) if inverse else (rr * 2 + half)
                r0 = u * NA_UNIT + src * NA_HALF
                chunks.append(a[r0:r0 + NA_HALF])
    return jnp.concatenate(chunks, axis=0)


def _in_proj_kernel(x_ref, g1_ref, w_ref, gcol_ref, o_ref, *, tile):
    n_tiles = x_ref.shape[1] // tile

    def norm(t):
        x = x_ref[0, t * tile:(t + 1) * tile, :]
        ms = jnp.mean(x * x, axis=-1, keepdims=True)
        h = (x * lax.rsqrt(ms + EPS) * g1_ref[...]).astype(jnp.bfloat16)
        return h, (_na_query_order(h) if tile % NA_UNIT == 0 else h)

    def project(hs, chunk):
        row0, rows, _, na_order = chunk
        p = jnp.dot(hs[1] if na_order else hs[0], w_ref[:, row0:row0 + rows],
                    preferred_element_type=jnp.float32)
        return p.T

    def head_norm_store(t, chunk, p):
        row0, rows, normed_rows, _ = chunk
        ts = slice(t * tile, (t + 1) * tile)
        if normed_rows < rows:
            o_ref[0, row0 + normed_rows:row0 + rows, ts] = p[normed_rows:].astype(jnp.bfloat16)
        for i in range(normed_rows // HEAD_DIM):
            blk = p[i * HEAD_DIM:(i + 1) * HEAD_DIM, :]
            ss = jnp.mean(blk * blk, axis=0, keepdims=True)
            r0 = row0 + i * HEAD_DIM
            y = blk * lax.rsqrt(ss + EPS) * gcol_ref[r0:r0 + HEAD_DIM, :]
            o_ref[0, r0:r0 + HEAD_DIM, ts] = y.astype(jnp.bfloat16)

    items = [(t, chunk) for t in range(n_tiles) for chunk in _PROJ_CHUNKS]
    hs = {0: norm(0)}
    nxt = project(hs[0], items[0][1])
    for idx, (t, chunk) in enumerate(items):
        p = nxt
        if chunk is _PROJ_CHUNKS[-3] and t + 1 < n_tiles:
            hs[t + 1] = norm(t + 1)
        if idx + 1 < len(items):
            t_next, chunk_next = items[idx + 1]
            nxt = project(hs[t_next], chunk_next)
        head_norm_store(t, chunk, p)


def _in_proj(x, g1, w_t, gcol, tm, tile):
    b, n, _ = x.shape
    assert n % tm == 0 and tm % tile == 0
    return pl.pallas_call(
        functools.partial(_in_proj_kernel, tile=tile),
        grid=(b, n // tm),
        in_specs=[
            pl.BlockSpec((1, tm, D_MODEL), lambda i, j: (i, j, 0)),
            pl.BlockSpec((1, D_MODEL), lambda i, j: (0, 0)),
            pl.BlockSpec((D_MODEL, IN_WIDTH), lambda i, j: (0, 0),
                         pipeline_mode=pl.Buffered(1)),
            pl.BlockSpec((IN_WIDTH, 1), lambda i, j: (0, 0)),
        ],
        out_specs=pl.BlockSpec((1, IN_WIDTH, tm), lambda i, j: (i, 0, j)),
        out_shape=jax.ShapeDtypeStruct((b, IN_WIDTH, n), jnp.bfloat16),
        compiler_params=_params(2),
        name="in_proj",
    )(x, g1, w_t, gcol)


def _fold8(x, op):
    acc = x[0:8]
    for r in range(8, x.shape[0], 8):
        acc = op(acc, x[r:r + 8])
    return acc


def _na_kernel(q_ref, kv0_ref, kv1_ref, kv2_ref, kvm_ref, t_ref, g_ref, o_ref, acc_ref):
    kv_refs = (kv0_ref, kv1_ref, kv2_ref)
    lanes = (slice(0, 2 * GRID_W), slice(2 * GRID_W, 4 * GRID_W))

    def scores(h):
        hs = slice(h * HEAD_DIM, (h + 1) * HEAD_DIM)
        q = q_ref[0, hs, :]
        blocks = []
        for c in range(NA_KEY_UNITS):
            s = lax.dot_general(kv_refs[c][0, hs, :], q, _TN,
                                preferred_element_type=jnp.float32)
            for jj in range(NA_UNIT_ROWS):
                j = NA_UNIT_ROWS * c + jj
                pair = []
                for half in range(2):
                    r0 = jj * GRID_W + half * NA_KOFF
                    pair.append(s[r0:r0 + NA_KSPAN, lanes[half]] + t_ref[0, h, j, half])
                blocks.append(pair)
        s_meta = lax.dot_general(kvm_ref[hs, :], q, _TN,
                                 preferred_element_type=jnp.float32)
        return blocks, s_meta

    def finish(h, blocks, s_meta):
        hs = slice(h * HEAD_DIM, (h + 1) * HEAD_DIM)
        vs = slice(NA_WIDTH + h * HEAD_DIM, NA_WIDTH + (h + 1) * HEAD_DIM)
        m8 = _fold8(s_meta, jnp.maximum)
        m8 = [m8[:, lanes[0]], m8[:, lanes[1]]]
        for pair in blocks:
            for half in range(2):
                m8[half] = jnp.maximum(m8[half], _fold8(pair[half], jnp.maximum))
        m = [jnp.max(m8[half], axis=0, keepdims=True) for half in range(2)]
        p_meta = jnp.exp2(s_meta - jnp.concatenate(m, axis=1))
        l8 = _fold8(p_meta, jnp.add)
        l8 = [l8[:, lanes[0]], l8[:, lanes[1]]]
        o = lax.dot_general(kvm_ref[vs, :], p_meta.astype(jnp.bfloat16), _NN,
                            preferred_element_type=jnp.float32)
        pad = jnp.zeros((NA_KOFF, 2 * GRID_W), jnp.float32)
        for c in range(NA_KEY_UNITS):
            p_rows = []
            for jj in range(NA_UNIT_ROWS):
                pair = blocks[NA_UNIT_ROWS * c + jj]
                p = [jnp.exp2(pair[half] - m[half]) for half in range(2)]
                for half in range(2):
                    l8[half] = l8[half] + _fold8(p[half], jnp.add)
                p_rows.append(jnp.concatenate(
                    [jnp.concatenate([p[0], pad], axis=0),
                     jnp.concatenate([pad, p[1]], axis=0)], axis=1))
            p_c = jnp.concatenate(p_rows, axis=0).astype(jnp.bfloat16)
            o = o + lax.dot_general(kv_refs[c][0, vs, :], p_c, _NN,
                                    preferred_element_type=jnp.float32)
        l = jnp.concatenate([jnp.sum(l8[half], axis=0, keepdims=True) for half in range(2)],
                            axis=1)
        acc_ref[hs, :] = o * (1.0 / l)

    nxt = scores(0)
    for h in range(NA_HEADS):
        cur = nxt
        if h + 1 < NA_HEADS:
            nxt = scores(h + 1)
        finish(h, *cur)

    full = acc_ref[...]
    ms = jnp.mean(full * full, axis=0, keepdims=True)
    o_ref[0] = (full * lax.rsqrt(ms + EPS) * g_ref[...]).astype(jnp.bfloat16)


def _na_slab_index(rows, u):
    n_units = rows // NA_UNIT_ROWS
    k_row0 = NA_UNIT_ROWS * min(max(u - 1, 0), n_units - NA_KEY_UNITS)
    idx = np.full((NA_KEY_UNITS * NA_UNIT_ROWS, NA_UNIT_ROWS), NA_NEG_SLAB, np.int32)
    for j in range(idx.shape[0]):
        for rr in range(NA_UNIT_ROWS):
            kr, r = k_row0 + j, NA_UNIT_ROWS * u + rr
            rs = min(max(r - NA_WIN_ROWS // 2, 0), rows - NA_WIN_ROWS)
            if rs <= kr < rs + NA_WIN_ROWS:
                idx[j, rr] = kr - r + NA_WIN_ROWS - 1
    return idx


def _na_variant(u, n_units):
    return jnp.where(u == 0, 0, jnp.where(u == n_units - 1, 2, 1))


def _na_variants(rows):
    n_units = rows // NA_UNIT_ROWS
    assert rows % NA_UNIT_ROWS == 0 and n_units >= NA_KEY_UNITS
    variants = np.stack([_na_slab_index(rows, u) for u in (0, 1, n_units - 1)])
    for u in range(n_units):
        v = 0 if u == 0 else (2 if u == n_units - 1 else 1)
        assert (_na_slab_index(rows, u) == variants[v]).all()
    return variants


def _na_attn(proj_t, kvm_t, bias, variants, g_col):
    b, _, n = proj_t.shape
    n_units = n // NA_UNIT
    assert (_na_variants(n // GRID_W) == variants).all()

    def q_map(i, u):
        return (i, QA0 // NA_WIDTH, u)

    def kv_map(c):
        def index_map(i, u):
            return (i, KA0 // (2 * NA_WIDTH), jnp.clip(u - 1, 0, n_units - NA_KEY_UNITS) + c)
        return index_map

    assert VA0 == KA0 + NA_WIDTH and KA0 % (2 * NA_WIDTH) == 0
    blk = (1, NA_WIDTH, NA_UNIT)
    in_specs = [pl.BlockSpec(blk, q_map)]
    in_specs += [pl.BlockSpec((1, 2 * NA_WIDTH, NA_UNIT), kv_map(c)) for c in range(NA_KEY_UNITS)]
    in_specs += [
        pl.BlockSpec((2 * NA_WIDTH, N_META), lambda i, u: (0, 0)),
        pl.BlockSpec((1,) + bias.shape[1:],
                     lambda i, u: (_na_variant(u, n_units), 0, 0, 0, 0, 0)),
        pl.BlockSpec((NA_WIDTH, 1), lambda i, u: (0, 0)),
    ]
    return pl.pallas_call(
        _na_kernel,
        grid=(b, n_units),
        in_specs=in_specs,
        out_specs=pl.BlockSpec(blk, lambda i, u: (i, 0, u)),
        out_shape=jax.ShapeDtypeStruct((b, NA_WIDTH, n), jnp.bfloat16),
        scratch_shapes=[pltpu.VMEM((NA_WIDTH, NA_UNIT), jnp.float32)],
        compiler_params=_params(2),
        name="na_attn",
    )(proj_t, proj_t, proj_t, proj_t, kvm_t, bias, g_col)


def _wg_kernel(q_ref, kvl_ref, kvm_ref, kvr_ref, kvmeta_ref, band_ref, bmeta_ref,
               sink_ref, g_ref, o_ref, acc_ref, *, n_steps):
    j = pl.program_id(1)
    first = j == 0
    last = j == n_steps - 1

    def piece(rows, i):
        if i == 0:
            return kvl_ref[0, rows, :]
        if i == WG_QB + 1:
            return kvr_ref[0, rows, :]
        return kvm_ref[0, rows, (i - 1) * BLOCK:i * BLOCK]

    def scores(g, qb):
        gs = slice(g * HEAD_DIM, (g + 1) * HEAD_DIM)
        q = jnp.concatenate(
            [q_ref[0, (g * WG_GROUP + hh) * HEAD_DIM:(g * WG_GROUP + hh + 1) * HEAD_DIM,
                   qb * BLOCK:(qb + 1) * BLOCK]
             for hh in range(WG_GROUP)], axis=1)
        s_blocks = []
        for c in range(3):
            s = lax.dot_general(piece(gs, qb + c), q, _TN,
                                preferred_element_type=jnp.float32)
            s = s + band_ref[g, c * BLOCK:(c + 1) * BLOCK, :]
            if c == 0 and qb == 0:
                s = s + jnp.where(first, NEG_INF, 0.0)
            if c == 2 and qb == WG_QB - 1:
                s = s + jnp.where(last, NEG_INF, 0.0)
            s_blocks.append(s)
        meta_slab = jnp.where(first, 0, 1) if qb == 0 else 1
        s_meta = lax.dot_general(kvmeta_ref[gs, :], q, _TN,
                                 preferred_element_type=jnp.float32) + bmeta_ref[g, meta_slab]
        return s_blocks, s_meta

    def finish(g, qb, s_blocks, s_meta):
        vs = slice(WG_KV_WIDTH + g * HEAD_DIM, WG_KV_WIDTH + (g + 1) * HEAD_DIM)
        sink = sink_ref[g]
        m8 = _fold8(s_meta, jnp.maximum)
        for s in s_blocks:
            m8 = jnp.maximum(m8, _fold8(s, jnp.maximum))
        m = jnp.maximum(sink, jnp.max(m8, axis=0, keepdims=True))
        p_meta = jnp.exp2(s_meta - m)
        l8 = _fold8(p_meta, jnp.add)
        o = lax.dot_general(kvmeta_ref[vs, :], p_meta.astype(jnp.bfloat16), _NN,
                            preferred_element_type=jnp.float32)
        for c in range(3):
            p = jnp.exp2(s_blocks[c] - m)
            l8 = l8 + _fold8(p, jnp.add)
            o = o + lax.dot_general(piece(vs, qb + c), p.astype(jnp.bfloat16), _NN,
                                    preferred_element_type=jnp.float32)
        l = jnp.exp2(sink - m) + jnp.sum(l8, axis=0, keepdims=True)
        o = o * (1.0 / l)
        for hh in range(WG_GROUP):
            r0 = (g * WG_GROUP + hh) * HEAD_DIM
            acc_ref[r0:r0 + HEAD_DIM, qb * BLOCK:(qb + 1) * BLOCK] = (
                o[:, hh * BLOCK:(hh + 1) * BLOCK])

    units = [(g, qb) for qb in range(WG_QB) for g in range(WG_KV_HEADS)]
    nxt = scores(*units[0])
    for i, unit in enumerate(units):
        cur = nxt
        if i + 1 < len(units):
            nxt = scores(*units[i + 1])
        finish(*unit, *cur)

    full = acc_ref[...]
    ms = jnp.mean(full * full, axis=0, keepdims=True)
    o_ref[0] = (full * lax.rsqrt(ms + EPS) * g_ref[...]).astype(jnp.bfloat16)


def _wg_attn(proj_t, kvm_t, band, bmeta, sink_row, g_col):
    b, _, n = proj_t.shape
    nblk = n // BLOCK
    n_steps = nblk // WG_QB
    assert nblk % WG_QB == 0
    assert VB0 == KB0 + WG_KV_WIDTH and KB0 % (2 * WG_KV_WIDTH) == 0
    kv_rows = KB0 // (2 * WG_KV_WIDTH)

    tq = WG_QB * BLOCK
    side_blk = (1, 2 * WG_KV_WIDTH, BLOCK)
    in_specs = [
        pl.BlockSpec((1, WG_WIDTH, tq), lambda i, j: (i, QB0 // WG_WIDTH, j)),
        pl.BlockSpec(side_blk, lambda i, j: (i, kv_rows, jnp.maximum(j * WG_QB - 1, 0))),
        pl.BlockSpec((1, 2 * WG_KV_WIDTH, tq), lambda i, j: (i, kv_rows, j)),
        pl.BlockSpec(side_blk,
                     lambda i, j: (i, kv_rows, jnp.minimum((j + 1) * WG_QB, nblk - 1))),
        pl.BlockSpec((2 * WG_KV_WIDTH, N_META), lambda i, j: (0, 0)),
        pl.BlockSpec(band.shape, lambda i, j: (0, 0, 0)),
        pl.BlockSpec(bmeta.shape, lambda i, j: (0, 0, 0, 0)),
        pl.BlockSpec(sink_row.shape, lambda i, j: (0, 0, 0)),
        pl.BlockSpec((WG_WIDTH, 1), lambda i, j: (0, 0)),
    ]
    return pl.pallas_call(
        functools.partial(_wg_kernel, n_steps=n_steps),
        grid=(b, n_steps),
        in_specs=in_specs,
        out_specs=pl.BlockSpec((1, WG_WIDTH, tq), lambda i, j: (i, 0, j)),
        out_shape=jax.ShapeDtypeStruct((b, WG_WIDTH, n), jnp.bfloat16),
        scratch_shapes=[pltpu.VMEM((WG_WIDTH, tq), jnp.float32)],
        compiler_params=_params(2),
        name="wg_attn",
    )(proj_t, proj_t, proj_t, proj_t, kvm_t, band, bmeta, sink_row, g_col)


def _out_ffn_kernel(ma_ref, mb_ref, x_ref, woa_ref, wob_ref, g2_ref,
                    wg_ref, wu_ref, wd_ref, o_ref):
    n_tiles = x_ref.shape[1] // FFN_TILE
    n_chunks = D_FF // FFN_FC

    def attn_norm(t):
        ts = slice(t * FFN_TILE, (t + 1) * FFN_TILE)
        attn = lax.dot_general(ma_ref[0, :, ts], woa_ref[...], _TN,
                               preferred_element_type=jnp.float32)
        attn = _na_query_order(attn, inverse=True)
        attn = attn + lax.dot_general(mb_ref[0, :, ts], wob_ref[...], _TN,
                                      preferred_element_type=jnp.float32)
        x1 = x_ref[0, ts, :] + attn
        ms = jnp.mean(x1 * x1, axis=-1, keepdims=True)
        return x1, (x1 * lax.rsqrt(ms + EPS) * g2_ref[...]).astype(jnp.bfloat16)

    def gate_up(h, c):
        cs = slice(c * FFN_FC, (c + 1) * FFN_FC)
        return (jnp.dot(h, wg_ref[:, cs], preferred_element_type=jnp.float32),
                jnp.dot(h, wu_ref[:, cs], preferred_element_type=jnp.float32))

    items = [(t, c) for t in range(n_tiles) for c in range(n_chunks)]
    x1, h, acc = {}, {}, {}
    x1[0], h[0] = attn_norm(0)
    nxt = gate_up(h[0], 0)
    for idx, (t, c) in enumerate(items):
        gate, up = nxt
        if c == n_chunks - FFN_LOOKAHEAD and t + 1 < n_tiles:
            x1[t + 1], h[t + 1] = attn_norm(t + 1)
        if idx + 1 < len(items):
            t_next, c_next = items[idx + 1]
            nxt = gate_up(h[t_next], c_next)
        act = (gate * jax.nn.sigmoid(gate) * up).astype(jnp.bfloat16)
        down = jnp.dot(act, wd_ref[c * FFN_FC:(c + 1) * FFN_FC, :],
                       preferred_element_type=jnp.float32)
        acc[t] = down if c == 0 else acc[t] + down
        if c == n_chunks - 1:
            o_ref[0, t * FFN_TILE:(t + 1) * FFN_TILE, :] = x1[t] + acc[t]


def _out_ffn(mix_a, mix_b, x, wo_a, wo_b, g2, w_gate, w_up, w_down, tm):
    b, n, _ = x.shape

    def const(shape):
        return pl.BlockSpec(shape, lambda i, j: (0,) * len(shape),
                            pipeline_mode=pl.Buffered(1))

    mix_spec = pl.BlockSpec((1, NA_WIDTH, tm), lambda i, j: (i, 0, j))
    x_spec = pl.BlockSpec((1, tm, D_MODEL), lambda i, j: (i, j, 0))
    return pl.pallas_call(
        _out_ffn_kernel,
        grid=(b, n // tm),
        in_specs=[mix_spec, mix_spec, x_spec,
                  const((NA_WIDTH, D_MODEL)), const((WG_WIDTH, D_MODEL)),
                  const((1, D_MODEL)),
                  const((D_MODEL, D_FF)), const((D_MODEL, D_FF)),
                  const((D_FF, D_MODEL))],
        out_specs=x_spec,
        out_shape=jax.ShapeDtypeStruct((b, n, D_MODEL), jnp.float32),
        compiler_params=_params(2),
        name="out_ffn",
    )(mix_a, mix_b, x, wo_a, wo_b, g2, w_gate, w_up, w_down)


def _t5_bucket(rel):
    half = T5_BUCKETS // 2
    max_exact = half // 2
    ret = jnp.where(rel > 0, half, 0)
    n = jnp.abs(rel)
    nf = jnp.maximum(n, 1).astype(jnp.float32)
    large = max_exact + (jnp.log(nf / max_exact) / math.log(T5_MAX_DIST / max_exact)
                         * (half - max_exact)).astype(jnp.int32)
    large = jnp.minimum(large, half - 1)
    return ret + jnp.where(n < max_exact, n, large)


def _select_rows(table, idx):
    onehot = idx[..., None] == jnp.arange(table.shape[0])
    return jnp.sum(jnp.where(onehot[..., None], table, 0.0), axis=-2)


def _na_bias_table(rpb, variants):
    n_var, n_j, n_rr = variants.shape
    n_drow, n_dcol = 2 * NA_WIN_ROWS - 1, 2 * NA_WIN_COLS - 1
    sel_row = (variants[..., None] == np.arange(n_drow)).astype(np.float32)
    sel_col = np.zeros((2, NA_KSPAN, NA_HALF, n_dcol), np.float32)
    for half in range(2):
        for k in range(NA_KSPAN):
            for cc in range(NA_HALF):
                kc, c = half * NA_KOFF + k, half * NA_HALF + cc
                cs = min(max(c - NA_WIN_COLS // 2, 0), GRID_W - NA_WIN_COLS)
                if cs <= kc < cs + NA_WIN_COLS:
                    sel_col[half, k, cc, kc - c + NA_WIN_COLS - 1] = 1.0
    sel_col = np.tile(sel_col, (1, 1, n_rr, 1))
    sel_row = np.repeat(sel_row, NA_HALF, axis=2)
    by_col = jnp.einsum("hde,akle->hadkl", rpb.astype(jnp.float32), sel_col,
                        precision=lax.Precision.HIGHEST)
    pick = sel_row.transpose(0, 1, 3, 2)[:, None, :, None, :, None, :]
    table = jnp.sum(pick * by_col[None, :, None], axis=4)
    inside = (sel_row.sum(-1)[:, None, :, None, None, :]
              * sel_col.sum(-1)[None, None, None, :, :, :]) > 0
    return jnp.where(inside, table, NEG_INF)


def _wg_band_bias(t5):
    j = jnp.arange(3 * BLOCK)[:, None]
    i = jnp.arange(BLOCK)[None, :]
    rel = j - BLOCK - i
    tb = jnp.where((jnp.abs(rel) <= WINDOW)[..., None], _select_rows(t5, _t5_bucket(rel)),
                   NEG_INF)
    tb = tb.transpose(2, 0, 1).reshape(WG_KV_HEADS, WG_GROUP, 3 * BLOCK, BLOCK)
    return tb.transpose(0, 2, 1, 3).reshape(WG_KV_HEADS, 3 * BLOCK, WG_GROUP * BLOCK)


def _wg_meta_bias(t5):
    assert BLOCK + 1 >= T5_MAX_DIST
    rel = jnp.arange(N_META)[None, :] - (N_META + jnp.arange(2 * BLOCK)[:, None])
    bm = _select_rows(t5, _t5_bucket(rel))
    bm = bm.reshape(2, BLOCK, N_META, WG_KV_HEADS, WG_GROUP)
    return bm.transpose(3, 0, 2, 4, 1).reshape(WG_KV_HEADS, 2, N_META, WG_GROUP * BLOCK)


def _encode(x, meta_proj, w_in_t, g1, gcol, na_bias, na_variants, ga_col, band, bmeta, sink_row,
            gb_col, wo_a, wo_b, g2, w_gate, w_up, w_down):
    proj_t = _in_proj(x, g1, w_in_t, gcol, PROJ_TM, PROJ_TILE)
    mix_a = _na_attn(proj_t, meta_proj[KA0:KA0 + 2 * NA_WIDTH], na_bias, na_variants, ga_col)
    mix_b = _wg_attn(proj_t, meta_proj[KB0:KB0 + 2 * WG_KV_WIDTH], band, bmeta, sink_row, gb_col)
    return _out_ffn(mix_a, mix_b, x, wo_a, wo_b, g2, w_gate, w_up, w_down, FFN_TM)


def kernel(x_prompt, x_sample, meta_tokens, t5_table, norm1_g, w_in, qn_a_g, kn_a_g, rpb_a,
           qn_b_g, kn_b_g, sink_b, outn_a_g, outn_b_g, w_out, norm2_g, w_gate, w_up, w_down):
    f32, bf16 = jnp.float32, jnp.bfloat16
    q_scale = HEAD_DIM ** -0.5 * LOG2E
    qa_w, ka_w, va_w, qb_w, kb_w, vb_w = jnp.split(w_in[0], SPLITS_IN, axis=1)
    w_in_t = jnp.concatenate([qa_w, qb_w, ka_w, va_w, kb_w, vb_w], axis=1).astype(bf16)
    g1 = norm1_g[0].astype(f32).reshape(1, D_MODEL)
    g2 = norm2_g[0].astype(f32).reshape(1, D_MODEL)
    ones = jnp.ones
    gcol = jnp.concatenate([
        jnp.tile(qn_a_g[0].astype(f32), NA_HEADS) * q_scale,
        jnp.tile(qn_b_g[0].astype(f32), WG_HEADS) * q_scale,
        jnp.tile(kn_a_g[0].astype(f32), NA_HEADS),
        ones((NA_WIDTH,), f32),
        jnp.tile(kn_b_g[0].astype(f32), WG_KV_HEADS),
        ones((WG_KV_WIDTH,), f32),
    ]).reshape(IN_WIDTH, 1)
    ga_col = outn_a_g[0].astype(f32).reshape(NA_WIDTH, 1)
    gb_col = outn_b_g[0].astype(f32).reshape(WG_WIDTH, 1)
    t5 = t5_table.astype(f32) * LOG2E
    na_variants = _na_variants(x_prompt.shape[1] // GRID_W)
    na_bias = _na_bias_table(rpb_a[0].astype(f32) * LOG2E, na_variants)
    band = _wg_band_bias(t5)
    bmeta = _wg_meta_bias(t5)
    sink_row = jnp.repeat(sink_b[0].astype(f32).reshape(WG_KV_HEADS, WG_GROUP) * LOG2E, BLOCK,
                          axis=1).reshape(WG_KV_HEADS, 1, WG_GROUP * BLOCK)
    wo = w_out[0].astype(bf16)
    wo_a, wo_b = wo[:NA_WIDTH], wo[NA_WIDTH:]
    wg, wu, wd = w_gate[0].astype(bf16), w_up[0].astype(bf16), w_down[0].astype(bf16)

    meta = jnp.zeros((1, META_PAD, D_MODEL), f32).at[0, :N_META].set(meta_tokens.astype(f32))
    meta_proj = _in_proj(meta, g1, w_in_t, gcol, META_PAD, META_PAD)[0, :, :N_META]

    def enc(x):
        return _encode(x, meta_proj, w_in_t, g1, gcol, na_bias, na_variants, ga_col, band, bmeta,
                       sink_row, gb_col, wo_a, wo_b, g2, wg, wu, wd)

    return (enc(x_prompt), enc(x_sample))
```

```python
import functools
import math

import jax
import jax.numpy as jnp
import numpy as np
from jax import lax
from jax.experimental import pallas as pl
from jax.experimental.pallas import tpu as pltpu

D_MODEL = 1024
HEAD_DIM = 64
NA_HEADS = 8
WG_HEADS = 8
WG_KV_HEADS = 2
WG_GROUP = WG_HEADS // WG_KV_HEADS
NA_WIDTH = NA_HEADS * HEAD_DIM
WG_WIDTH = WG_HEADS * HEAD_DIM
WG_KV_WIDTH = WG_KV_HEADS * HEAD_DIM
IN_WIDTH = 3 * NA_WIDTH + WG_WIDTH + 2 * WG_KV_WIDTH
D_FF = 2816
GRID_W = 64
NA_WIN_ROWS = 8
NA_WIN_COLS = 16
N_META = 16
WINDOW = 128
BLOCK = 128
T5_BUCKETS = 32
T5_MAX_DIST = 128
EPS = 1e-6
NEG_INF = -1e30

QA0 = 0
QB0 = QA0 + NA_WIDTH
KA0 = QB0 + WG_WIDTH
VA0 = KA0 + NA_WIDTH
KB0 = VA0 + NA_WIDTH
VB0 = KB0 + WG_KV_WIDTH
LOG2E = math.log2(math.e)
SPLITS_IN = (NA_WIDTH, 2 * NA_WIDTH, 3 * NA_WIDTH, 3 * NA_WIDTH + WG_WIDTH,
             3 * NA_WIDTH + WG_WIDTH + WG_KV_WIDTH)

NA_UNIT_ROWS = 4
NA_UNIT = NA_UNIT_ROWS * GRID_W
NA_KEY_UNITS = 3
NA_STEP_UNITS = 2
NA_NEG_SLAB = 2 * NA_WIN_ROWS - 1
NA_HALF = GRID_W // 2
NA_KSPAN = NA_HALF + NA_WIN_COLS // 2
NA_KOFF = GRID_W - NA_KSPAN
WG_QB = 8

PROJ_TM = 1024
PROJ_TILE = 256
FFN_TM = 512
FFN_TILE = 256
FFN_FC = 256
FFN_LOOKAHEAD = 4
META_PAD = 128

VMEM_LIMIT_BYTES = 56 * 1024 * 1024

_TN = (((0,), (0,)), ((), ()))
_NT = (((1,), (1,)), ((), ()))
_NN = (((1,), (0,)), ((), ()))


def _params(n_grid_dims):
    return pltpu.CompilerParams(
        dimension_semantics=("arbitrary",) * n_grid_dims,
        vmem_limit_bytes=VMEM_LIMIT_BYTES)


_PROJ_CHUNKS = (
    (QA0, NA_WIDTH, NA_WIDTH, True),
    (QB0, WG_WIDTH, WG_WIDTH, False),
    (KA0, NA_WIDTH, NA_WIDTH, False),
    (VA0, NA_WIDTH, 0, False),
    (KB0, 2 * WG_KV_WIDTH, WG_KV_WIDTH, False),
)


def _na_query_order(a, inverse=False):
    chunks = []
    for u in range(a.shape[0] // NA_UNIT):
        for outer in range(NA_UNIT_ROWS if inverse else 2):
            for inner in range(2 if inverse else NA_UNIT_ROWS):
                rr, half = (outer, inner) if inverse else (inner, outer)
                src = (half * NA_UNIT_ROWS + rr) if inverse else (rr * 2 + half)
                r0 = u * NA_UNIT + src * NA_HALF
                chunks.append(a[r0:r0 + NA_HALF])
    return jnp.concatenate(chunks, axis=0)


def _in_proj_kernel(x_ref, g1_ref, w_ref, gcol_ref, o_ref, *, tile):
    n_tiles = x_ref.shape[1] // tile

    def norm(t):
        x = x_ref[0, t * tile:(t + 1) * tile, :]
        ms = jnp.mean(x * x, axis=-1, keepdims=True)
        h = (x * lax.rsqrt(ms + EPS) * g1_ref[...]).astype(jnp.bfloat16)
        return h, (_na_query_order(h) if tile % NA_UNIT == 0 else h)

    def project(hs, chunk):
        row0, rows, _, na_order = chunk
        p = jnp.dot(hs[1] if na_order else hs[0], w_ref[:, row0:row0 + rows],
                    preferred_element_type=jnp.float32)
        return p.T

    def head_norm_store(t, chunk, p):
        row0, rows, normed_rows, _ = chunk
        ts = slice(t * tile, (t + 1) * tile)
        if normed_rows < rows:
            o_ref[0, row0 + normed_rows:row0 + rows, ts] = p[normed_rows:].astype(jnp.bfloat16)
        for i in range(normed_rows // HEAD_DIM):
            blk = p[i * HEAD_DIM:(i + 1) * HEAD_DIM, :]
            ss = jnp.mean(blk * blk, axis=0, keepdims=True)
            r0 = row0 + i * HEAD_DIM
            y = blk * lax.rsqrt(ss + EPS) * gcol_ref[r0:r0 + HEAD_DIM, :]
            o_ref[0, r0:r0 + HEAD_DIM, ts] = y.astype(jnp.bfloat16)

    items = [(t, chunk) for t in range(n_tiles) for chunk in _PROJ_CHUNKS]
    hs = {0: norm(0)}
    nxt = project(hs[0], items[0][1])
    for idx, (t, chunk) in enumerate(items):
        p = nxt
        if chunk is _PROJ_CHUNKS[-3] and t + 1 < n_tiles:
            hs[t + 1] = norm(t + 1)
        if idx + 1 < len(items):
            t_next, chunk_next = items[idx + 1]
            nxt = project(hs[t_next], chunk_next)
        head_norm_store(t, chunk, p)


def _in_proj(x, g1, w_t, gcol, tm, tile):
    b, n, _ = x.shape
    assert n % tm == 0 and tm % tile == 0
    return pl.pallas_call(
        functools.partial(_in_proj_kernel, tile=tile),
        grid=(b, n // tm),
        in_specs=[
            pl.BlockSpec((1, tm, D_MODEL), lambda i, j: (i, j, 0)),
            pl.BlockSpec((1, D_MODEL), lambda i, j: (0, 0)),
            pl.BlockSpec((D_MODEL, IN_WIDTH), lambda i, j: (0, 0),
                         pipeline_mode=pl.Buffered(1)),
            pl.BlockSpec((IN_WIDTH, 1), lambda i, j: (0, 0)),
        ],
        out_specs=pl.BlockSpec((1, IN_WIDTH, tm), lambda i, j: (i, 0, j)),
        out_shape=jax.ShapeDtypeStruct((b, IN_WIDTH, n), jnp.bfloat16),
        compiler_params=_params(2),
        name="in_proj",
    )(x, g1, w_t, gcol)


def _fold8(x, op):
    acc = x[0:8]
    for r in range(8, x.shape[0], 8):
        acc = op(acc, x[r:r + 8])
    return acc


def _na_window(step, n_units):
    lo = jnp.clip(NA_STEP_UNITS * step - 1, 0, n_units - (NA_STEP_UNITS + 2))
    starts = [jnp.clip(NA_STEP_UNITS * step + x - 1, 0, n_units - NA_KEY_UNITS) - lo
              for x in range(NA_STEP_UNITS)]
    return lo, starts


def _na_kernel(q_ref, kv0_ref, kv1_ref, kv2_ref, kv3_ref, kvm_ref, t0_ref, t1_ref,
               g_ref, o_ref, acc_ref, *, n_units):
    kv_refs = (kv0_ref, kv1_ref, kv2_ref, kv3_ref)
    t_refs = (t0_ref, t1_ref)
    lanes = (slice(0, 2 * GRID_W), slice(2 * GRID_W, 4 * GRID_W))
    _, starts = _na_window(pl.program_id(1), n_units)

    def kv_block(x, c, rows):
        return jnp.where(starts[x] == 1, kv_refs[c + 1][0, rows, :], kv_refs[c][0, rows, :])

    def scores(x, h):
        hs = slice(h * HEAD_DIM, (h + 1) * HEAD_DIM)
        q = q_ref[0, hs, x * NA_UNIT:(x + 1) * NA_UNIT]
        blocks = []
        for c in range(NA_KEY_UNITS):
            s = lax.dot_general(kv_block(x, c, hs), q, _TN,
                                preferred_element_type=jnp.float32)
            for jj in range(NA_UNIT_ROWS):
                j = NA_UNIT_ROWS * c + jj
                pair = []
                for half in range(2):
                    r0 = jj * GRID_W + half * NA_KOFF
                    pair.append(s[r0:r0 + NA_KSPAN, lanes[half]] + t_refs[x][0, h, j, half])
                blocks.append(pair)
        s_meta = lax.dot_general(kvm_ref[hs, :], q, _TN,
                                 preferred_element_type=jnp.float32)
        return blocks, s_meta

    def finish(x, h, blocks, s_meta):
        hs = slice(h * HEAD_DIM, (h + 1) * HEAD_DIM)
        vs = slice(NA_WIDTH + h * HEAD_DIM, NA_WIDTH + (h + 1) * HEAD_DIM)
        m8 = _fold8(s_meta, jnp.maximum)
        m8 = [m8[:, lanes[0]], m8[:, lanes[1]]]
        for pair in blocks:
            for half in range(2):
                m8[half] = jnp.maximum(m8[half], _fold8(pair[half], jnp.maximum))
        m = [jnp.max(m8[half], axis=0, keepdims=True) for half in range(2)]
        p_meta = jnp.exp2(s_meta - jnp.concatenate(m, axis=1))
        l8 = _fold8(p_meta, jnp.add)
        l8 = [l8[:, lanes[0]], l8[:, lanes[1]]]
        o = lax.dot_general(kvm_ref[vs, :], p_meta.astype(jnp.bfloat16), _NN,
                            preferred_element_type=jnp.float32)
        pad = jnp.zeros((NA_KOFF, 2 * GRID_W), jnp.float32)
        for c in range(NA_KEY_UNITS):
            p_rows = []
            for jj in range(NA_UNIT_ROWS):
                pair = blocks[NA_UNIT_ROWS * c + jj]
                p = [jnp.exp2(pair[half] - m[half]) for half in range(2)]
                for half in range(2):
                    l8[half] = l8[half] + _fold8(p[half], jnp.add)
                p_rows.append(jnp.concatenate(
                    [jnp.concatenate([p[0], pad], axis=0),
                     jnp.concatenate([pad, p[1]], axis=0)], axis=1))
            p_c = jnp.concatenate(p_rows, axis=0).astype(jnp.bfloat16)
            o = o + lax.dot_general(kv_block(x, c, vs), p_c, _NN,
                                    preferred_element_type=jnp.float32)
        l = jnp.concatenate([jnp.sum(l8[half], axis=0, keepdims=True) for half in range(2)],
                            axis=1)
        acc_ref[hs, x * NA_UNIT:(x + 1) * NA_UNIT] = o * (1.0 / l)

    items = [(x, h) for x in range(NA_STEP_UNITS) for h in range(NA_HEADS)]
    nxt = scores(*items[0])
    for i, item in enumerate(items):
        cur = nxt
        if i + 1 < len(items):
            nxt = scores(*items[i + 1])
        finish(*item, *cur)

    full = acc_ref[...]
    ms = jnp.mean(full * full, axis=0, keepdims=True)
    o_ref[0] = (full * lax.rsqrt(ms + EPS) * g_ref[...]).astype(jnp.bfloat16)


def _na_slab_index(rows, u):
    n_units = rows // NA_UNIT_ROWS
    k_row0 = NA_UNIT_ROWS * min(max(u - 1, 0), n_units - NA_KEY_UNITS)
    idx = np.full((NA_KEY_UNITS * NA_UNIT_ROWS, NA_UNIT_ROWS), NA_NEG_SLAB, np.int32)
    for j in range(idx.shape[0]):
        for rr in range(NA_UNIT_ROWS):
            kr, r = k_row0 + j, NA_UNIT_ROWS * u + rr
            rs = min(max(r - NA_WIN_ROWS // 2, 0), rows - NA_WIN_ROWS)
            if rs <= kr < rs + NA_WIN_ROWS:
                idx[j, rr] = kr - r + NA_WIN_ROWS - 1
    return idx


def _na_variant(u, n_units):
    return jnp.where(u == 0, 0, jnp.where(u == n_units - 1, 2, 1))


def _na_variants(rows):
    n_units = rows // NA_UNIT_ROWS
    assert rows % NA_UNIT_ROWS == 0 and n_units >= NA_KEY_UNITS
    variants = np.stack([_na_slab_index(rows, u) for u in (0, 1, n_units - 1)])
    for u in range(n_units):
        v = 0 if u == 0 else (2 if u == n_units - 1 else 1)
        assert (_na_slab_index(rows, u) == variants[v]).all()
    return variants


def _na_attn(proj_t, kvm_t, bias, variants, g_col):
    b, _, n = proj_t.shape
    n_units = n // NA_UNIT
    assert (_na_variants(n // GRID_W) == variants).all()
    assert NA_STEP_UNITS == 2 and n_units % NA_STEP_UNITS == 0 and n_units >= NA_STEP_UNITS + 2
    assert VA0 == KA0 + NA_WIDTH and KA0 % (2 * NA_WIDTH) == 0

    def kv_map(c):
        return lambda i, s: (i, KA0 // (2 * NA_WIDTH), _na_window(s, n_units)[0] + c)

    def bias_map(x):
        return lambda i, s: (_na_variant(NA_STEP_UNITS * s + x, n_units), 0, 0, 0, 0, 0)

    blk = (1, NA_WIDTH, NA_STEP_UNITS * NA_UNIT)
    in_specs = [pl.BlockSpec(blk, lambda i, s: (i, QA0 // NA_WIDTH, s))]
    in_specs += [pl.BlockSpec((1, 2 * NA_WIDTH, NA_UNIT), kv_map(c))
                 for c in range(NA_STEP_UNITS + 2)]
    in_specs += [pl.BlockSpec((2 * NA_WIDTH, N_META), lambda i, s: (0, 0))]
    in_specs += [pl.BlockSpec((1,) + bias.shape[1:], bias_map(x)) for x in range(NA_STEP_UNITS)]
    in_specs += [pl.BlockSpec((NA_WIDTH, 1), lambda i, s: (0, 0))]
    return pl.pallas_call(
        functools.partial(_na_kernel, n_units=n_units),
        grid=(b, n_units // NA_STEP_UNITS),
        in_specs=in_specs,
        out_specs=pl.BlockSpec(blk, lambda i, s: (i, 0, s)),
        out_shape=jax.ShapeDtypeStruct((b, NA_WIDTH, n), jnp.bfloat16),
        scratch_shapes=[pltpu.VMEM((NA_WIDTH, NA_STEP_UNITS * NA_UNIT), jnp.float32)],
        compiler_params=_params(2),
        name="na_attn",
    )(proj_t, proj_t, proj_t, proj_t, proj_t, kvm_t, bias, bias, g_col)


def _wg_kernel(q_ref, kvl_ref, kvm_ref, kvr_ref, kvmeta_ref, band_ref, bmeta_ref,
               sink_ref, g_ref, o_ref, acc_ref, *, n_steps):
    j = pl.program_id(1)
    first = j == 0
    last = j == n_steps - 1

    def piece(rows, i):
        if i == 0:
            return kvl_ref[0, rows, :]
        if i == WG_QB + 1:
            return kvr_ref[0, rows, :]
        return kvm_ref[0, rows, (i - 1) * BLOCK:i * BLOCK]

    def scores(g, qb):
        gs = slice(g * HEAD_DIM, (g + 1) * HEAD_DIM)
        q = jnp.concatenate(
            [q_ref[0, (g * WG_GROUP + hh) * HEAD_DIM:(g * WG_GROUP + hh + 1) * HEAD_DIM,
                   qb * BLOCK:(qb + 1) * BLOCK]
             for hh in range(WG_GROUP)], axis=1)
        s_blocks = []
        for c in range(3):
            s = lax.dot_general(piece(gs, qb + c), q, _TN,
                                preferred_element_type=jnp.float32)
            s = s + band_ref[g, c * BLOCK:(c + 1) * BLOCK, :]
            if c == 0 and qb == 0:
                s = s + jnp.where(first, NEG_INF, 0.0)
            if c == 2 and qb == WG_QB - 1:
                s = s + jnp.where(last, NEG_INF, 0.0)
            s_blocks.append(s)
        meta_slab = jnp.where(first, 0, 1) if qb == 0 else 1
        s_meta = lax.dot_general(kvmeta_ref[gs, :], q, _TN,
                                 preferred_element_type=jnp.float32) + bmeta_ref[g, meta_slab]
        return s_blocks, s_meta

    def finish(g, qb, s_blocks, s_meta):
        vs = slice(WG_KV_WIDTH + g * HEAD_DIM, WG_KV_WIDTH + (g + 1) * HEAD_DIM)
        sink = sink_ref[g]
        m8 = _fold8(s_meta, jnp.maximum)
        for s in s_blocks:
            m8 = jnp.maximum(m8, _fold8(s, jnp.maximum))
        m = jnp.maximum(sink, jnp.max(m8, axis=0, keepdims=True))
        p_meta = jnp.exp2(s_meta - m)
        l8 = _fold8(p_meta, jnp.add)
        o = lax.dot_general(kvmeta_ref[vs, :], p_meta.astype(jnp.bfloat16), _NN,
                            preferred_element_type=jnp.float32)
        for c in range(3):
            p = jnp.exp2(s_blocks[c] - m)
            l8 = l8 + _fold8(p, jnp.add)
            o = o + lax.dot_general(piece(vs, qb + c), p.astype(jnp.bfloat16), _NN,
                                    preferred_element_type=jnp.float32)
        l = jnp.exp2(sink - m) + jnp.sum(l8, axis=0, keepdims=True)
        o = o * (1.0 / l)
        for hh in range(WG_GROUP):
            r0 = (g * WG_GROUP + hh) * HEAD_DIM
            acc_ref[r0:r0 + HEAD_DIM, qb * BLOCK:(qb + 1) * BLOCK] = (
                o[:, hh * BLOCK:(hh + 1) * BLOCK])

    units = [(g, qb) for qb in range(WG_QB) for g in range(WG_KV_HEADS)]
    nxt = scores(*units[0])
    for i, unit in enumerate(units):
        cur = nxt
        if i + 1 < len(units):
            nxt = scores(*units[i + 1])
        finish(*unit, *cur)

    full = acc_ref[...]
    ms = jnp.mean(full * full, axis=0, keepdims=True)
    o_ref[0] = (full * lax.rsqrt(ms + EPS) * g_ref[...]).astype(jnp.bfloat16)


def _wg_attn(proj_t, kvm_t, band, bmeta, sink_row, g_col):
    b, _, n = proj_t.shape
    nblk = n // BLOCK
    n_steps = nblk // WG_QB
    assert nblk % WG_QB == 0
    assert VB0 == KB0 + WG_KV_WIDTH and KB0 % (2 * WG_KV_WIDTH) == 0
    kv_rows = KB0 // (2 * WG_KV_WIDTH)

    tq = WG_QB * BLOCK
    side_blk = (1, 2 * WG_KV_WIDTH, BLOCK)
    in_specs = [
        pl.BlockSpec((1, WG_WIDTH, tq), lambda i, j: (i, QB0 // WG_WIDTH, j)),
        pl.BlockSpec(side_blk, lambda i, j: (i, kv_rows, jnp.maximum(j * WG_QB - 1, 0))),
        pl.BlockSpec((1, 2 * WG_KV_WIDTH, tq), lambda i, j: (i, kv_rows, j)),
        pl.BlockSpec(side_blk,
                     lambda i, j: (i, kv_rows, jnp.minimum((j + 1) * WG_QB, nblk - 1))),
        pl.BlockSpec((2 * WG_KV_WIDTH, N_META), lambda i, j: (0, 0)),
        pl.BlockSpec(band.shape, lambda i, j: (0, 0, 0)),
        pl.BlockSpec(bmeta.shape, lambda i, j: (0, 0, 0, 0)),
        pl.BlockSpec(sink_row.shape, lambda i, j: (0, 0, 0)),
        pl.BlockSpec((WG_WIDTH, 1), lambda i, j: (0, 0)),
    ]
    return pl.pallas_call(
        functools.partial(_wg_kernel, n_steps=n_steps),
        grid=(b, n_steps),
        in_specs=in_specs,
        out_specs=pl.BlockSpec((1, WG_WIDTH, tq), lambda i, j: (i, 0, j)),
        out_shape=jax.ShapeDtypeStruct((b, WG_WIDTH, n), jnp.bfloat16),
        scratch_shapes=[pltpu.VMEM((WG_WIDTH, tq), jnp.float32)],
        compiler_params=_params(2),
        name="wg_attn",
    )(proj_t, proj_t, proj_t, proj_t, kvm_t, band, bmeta, sink_row, g_col)


def _out_ffn_kernel(ma_ref, mb_ref, x_ref, woa_ref, wob_ref, g2_ref,
                    wg_ref, wu_ref, wd_ref, o_ref):
    n_tiles = x_ref.shape[1] // FFN_TILE
    n_chunks = D_FF // FFN_FC

    def attn_norm(t):
        ts = slice(t * FFN_TILE, (t + 1) * FFN_TILE)
        attn = lax.dot_general(ma_ref[0, :, ts], woa_ref[...], _TN,
                               preferred_element_type=jnp.float32)
        attn = _na_query_order(attn, inverse=True)
        attn = attn + lax.dot_general(mb_ref[0, :, ts], wob_ref[...], _TN,
                                      preferred_element_type=jnp.float32)
        x1 = x_ref[0, ts, :] + attn
        ms = jnp.mean(x1 * x1, axis=-1, keepdims=True)
        return x1, (x1 * lax.rsqrt(ms + EPS) * g2_ref[...]).astype(jnp.bfloat16)

    def gate_up(h, c):
        cs = slice(c * FFN_FC, (c + 1) * FFN_FC)
        return (jnp.dot(h, wg_ref[:, cs], preferred_element_type=jnp.float32),
                jnp.dot(h, wu_ref[:, cs], preferred_element_type=jnp.float32))

    items = [(t, c) for t in range(n_tiles) for c in range(n_chunks)]
    x1, h, acc = {}, {}, {}
    x1[0], h[0] = attn_norm(0)
    nxt = gate_up(h[0], 0)
    for idx, (t, c) in enumerate(items):
        gate, up = nxt
        if c == n_chunks - FFN_LOOKAHEAD and t + 1 < n_tiles:
            x1[t + 1], h[t + 1] = attn_norm(t + 1)
        if idx + 1 < len(items):
            t_next, c_next = items[idx + 1]
            nxt = gate_up(h[t_next], c_next)
        act = (gate * jax.nn.sigmoid(gate) * up).astype(jnp.bfloat16)
        down = jnp.dot(act, wd_ref[c * FFN_FC:(c + 1) * FFN_FC, :],
                       preferred_element_type=jnp.float32)
        acc[t] = down if c == 0 else acc[t] + down
        if c == n_chunks - 1:
            o_ref[0, t * FFN_TILE:(t + 1) * FFN_TILE, :] = x1[t] + acc[t]


def _out_ffn(mix_a, mix_b, x, wo_a, wo_b, g2, w_gate, w_up, w_down, tm):
    b, n, _ = x.shape

    def const(shape):
        return pl.BlockSpec(shape, lambda i, j: (0,) * len(shape),
                            pipeline_mode=pl.Buffered(1))

    mix_spec = pl.BlockSpec((1, NA_WIDTH, tm), lambda i, j: (i, 0, j))
    x_spec = pl.BlockSpec((1, tm, D_MODEL), lambda i, j: (i, j, 0))
    return pl.pallas_call(
        _out_ffn_kernel,
        grid=(b, n // tm),
        in_specs=[mix_spec, mix_spec, x_spec,
                  const((NA_WIDTH, D_MODEL)), const((WG_WIDTH, D_MODEL)),
                  const((1, D_MODEL)),
                  const((D_MODEL, D_FF)), const((D_MODEL, D_FF)),
                  const((D_FF, D_MODEL))],
        out_specs=x_spec,
        out_shape=jax.ShapeDtypeStruct((b, n, D_MODEL), jnp.float32),
        compiler_params=_params(2),
        name="out_ffn",
    )(mix_a, mix_b, x, wo_a, wo_b, g2, w_gate, w_up, w_down)


def _t5_bucket(rel):
    half = T5_BUCKETS // 2
    max_exact = half // 2
    ret = jnp.where(rel > 0, half, 0)
    n = jnp.abs(rel)
    nf = jnp.maximum(n, 1).astype(jnp.float32)
    large = max_exact + jnp.trunc(jnp.log(nf / max_exact) / math.log(T5_MAX_DIST / max_exact)
                                  * (half - max_exact)).astype(jnp.int32)
    large = jnp.minimum(large, half - 1)
    return ret + jnp.where(n < max_exact, n, large)


def _na_bias_table(rpb, variants):
    n_var, n_j, n_rr = variants.shape
    n_drow, n_dcol = 2 * NA_WIN_ROWS - 1, 2 * NA_WIN_COLS - 1
    sel_row = (variants[..., None] == np.arange(n_drow)).astype(np.float32)
    sel_col = np.zeros((2, NA_KSPAN, NA_HALF, n_dcol), np.float32)
    for half in range(2):
        for k in range(NA_KSPAN):
            for cc in range(NA_HALF):
                kc, c = half * NA_KOFF + k, half * NA_HALF + cc
                cs = min(max(c - NA_WIN_COLS // 2, 0), GRID_W - NA_WIN_COLS)
                if cs <= kc < cs + NA_WIN_COLS:
                    sel_col[half, k, cc, kc - c + NA_WIN_COLS - 1] = 1.0
    sel_col = np.tile(sel_col, (1, 1, n_rr, 1))
    sel_row = np.repeat(sel_row, NA_HALF, axis=2)
    by_col = jnp.einsum("hde,akle->hadkl", rpb.astype(jnp.float32), sel_col,
                        precision=lax.Precision.HIGHEST)
    by_col = jnp.where(sel_col.sum(-1)[None, :, None] > 0, by_col, NEG_INF)
    by_col = jnp.concatenate([by_col, jnp.full_like(by_col[:, :, :1], NEG_INF)], axis=2)
    sel_row = np.concatenate([sel_row, 1.0 - sel_row.sum(-1, keepdims=True)], axis=-1)
    pick = sel_row.transpose(0, 1, 3, 2)[:, None, :, None, :, None, :]
    return jnp.sum(pick * by_col[None, :, None], axis=4)


def _wg_band_bias(t5):
    j = jnp.arange(3 * BLOCK)[:, None]
    i = jnp.arange(BLOCK)[None, :]
    rel = j - BLOCK - i
    tb = jnp.where((jnp.abs(rel) <= WINDOW)[..., None], t5[_t5_bucket(rel)], NEG_INF)
    tb = tb.transpose(2, 0, 1).reshape(WG_KV_HEADS, WG_GROUP, 3 * BLOCK, BLOCK)
    return tb.transpose(0, 2, 1, 3).reshape(WG_KV_HEADS, 3 * BLOCK, WG_GROUP * BLOCK)


def _wg_meta_bias(t5):
    assert BLOCK + 1 >= T5_MAX_DIST
    rel = jnp.arange(N_META)[None, :] - (N_META + jnp.arange(2 * BLOCK)[:, None])
    bm = t5[_t5_bucket(rel)]
    bm = bm.reshape(2, BLOCK, N_META, WG_KV_HEADS, WG_GROUP)
    return bm.transpose(3, 0, 2, 4, 1).reshape(WG_KV_HEADS, 2, N_META, WG_GROUP * BLOCK)


def _encode(x, meta_proj, w_in_t, g1, gcol, na_bias, na_variants, ga_col, band, bmeta, sink_row,
            gb_col, wo_a, wo_b, g2, w_gate, w_up, w_down):
    proj_t = _in_proj(x, g1, w_in_t, gcol, PROJ_TM, PROJ_TILE)
    mix_a = _na_attn(proj_t, meta_proj[KA0:KA0 + 2 * NA_WIDTH], na_bias, na_variants, ga_col)
    mix_b = _wg_attn(proj_t, meta_proj[KB0:KB0 + 2 * WG_KV_WIDTH], band, bmeta, sink_row, gb_col)
    return _out_ffn(mix_a, mix_b, x, wo_a, wo_b, g2, w_gate, w_up, w_down, FFN_TM)


def kernel(x_prompt, x_sample, meta_tokens, t5_table, norm1_g, w_in, qn_a_g, kn_a_g, rpb_a,
           qn_b_g, kn_b_g, sink_b, outn_a_g, outn_b_g, w_out, norm2_g, w_gate, w_up, w_down):
    f32, bf16 = jnp.float32, jnp.bfloat16
    q_scale = HEAD_DIM ** -0.5 * LOG2E
    qa_w, ka_w, va_w, qb_w, kb_w, vb_w = jnp.split(w_in[0], SPLITS_IN, axis=1)
    w_in_t = jnp.concatenate([qa_w, qb_w, ka_w, va_w, kb_w, vb_w], axis=1).astype(bf16)
    g1 = norm1_g[0].astype(f32).reshape(1, D_MODEL)
    g2 = norm2_g[0].astype(f32).reshape(1, D_MODEL)
    ones = jnp.ones
    gcol = jnp.concatenate([
        jnp.tile(qn_a_g[0].astype(f32), NA_HEADS) * q_scale,
        jnp.tile(qn_b_g[0].astype(f32), WG_HEADS) * q_scale,
        jnp.tile(kn_a_g[0].astype(f32), NA_HEADS),
        ones((NA_WIDTH,), f32),
        jnp.tile(kn_b_g[0].astype(f32), WG_KV_HEADS),
        ones((WG_KV_WIDTH,), f32),
    ]).reshape(IN_WIDTH, 1)
    ga_col = outn_a_g[0].astype(f32).reshape(NA_WIDTH, 1)
    gb_col = outn_b_g[0].astype(f32).reshape(WG_WIDTH, 1)
    t5 = t5_table.astype(f32) * LOG2E
    na_variants = _na_variants(x_prompt.shape[1] // GRID_W)
    na_bias = _na_bias_table(rpb_a[0].astype(f32) * LOG2E, na_variants)
    band = _wg_band_bias(t5)
    bmeta = _wg_meta_bias(t5)
    sink_row = jnp.repeat(sink_b[0].astype(f32).reshape(WG_KV_HEADS, WG_GROUP) * LOG2E, BLOCK,
                          axis=1).reshape(WG_KV_HEADS, 1, WG_GROUP * BLOCK)
    wo = w_out[0].astype(bf16)
    wo_a, wo_b = wo[:NA_WIDTH], wo[NA_WIDTH:]
    wg, wu, wd = w_gate[0].astype(bf16), w_up[0].astype(bf16), w_down[0].astype(bf16)

    meta = jnp.zeros((1, META_PAD, D_MODEL), f32).at[0, :N_META].set(meta_tokens.astype(f32))
    meta_proj = _in_proj(meta, g1, w_in_t, gcol, META_PAD, META_PAD)[0, :, :N_META]

    def enc(x):
        return _encode(x, meta_proj, w_in_t, g1, gcol, na_bias, na_variants, ga_col, band, bmeta,
                       sink_row, gb_col, wo_a, wo_b, g2, wg, wu, wd)

    return (enc(x_prompt), enc(x_sample))
```

```python
import functools
import math

import jax
import jax.numpy as jnp
import numpy as np
from jax import lax
from jax.experimental import pallas as pl
from jax.experimental.pallas import tpu as pltpu

D_MODEL = 1024
HEAD_DIM = 64
NA_HEADS = 8
WG_HEADS = 8
WG_KV_HEADS = 2
WG_GROUP = WG_HEADS // WG_KV_HEADS
NA_WIDTH = NA_HEADS * HEAD_DIM
WG_WIDTH = WG_HEADS * HEAD_DIM
WG_KV_WIDTH = WG_KV_HEADS * HEAD_DIM
IN_WIDTH = 3 * NA_WIDTH + WG_WIDTH + 2 * WG_KV_WIDTH
D_FF = 2816
GRID_W = 64
NA_WIN_ROWS = 8
NA_WIN_COLS = 16
N_META = 16
WINDOW = 128
BLOCK = 128
T5_BUCKETS = 32
T5_MAX_DIST = 128
EPS = 1e-6
NEG_INF = -1e30

QA0 = 0
QB0 = QA0 + NA_WIDTH
KA0 = QB0 + WG_WIDTH
VA0 = KA0 + NA_WIDTH
KB0 = VA0 + NA_WIDTH
VB0 = KB0 + WG_KV_WIDTH
LOG2E = math.log2(math.e)

NA_UNIT_ROWS = 4
NA_UNIT = NA_UNIT_ROWS * GRID_W
NA_KEY_UNITS = 3
NA_STEP_UNITS = 2
NA_NEG_SLAB = 2 * NA_WIN_ROWS - 1
NA_HALF = GRID_W // 2
NA_KSPAN = NA_HALF + NA_WIN_COLS // 2
NA_KOFF = GRID_W - NA_KSPAN
WG_QB = 8

PROJ_TM = 1024
PROJ_TILE = 256
FFN_TM = 512
FFN_TILE = 256
FFN_FC = 256
FFN_LOOKAHEAD = 4
META_PAD = 128

VMEM_LIMIT_BYTES = 56 * 1024 * 1024

_TN = (((0,), (0,)), ((), ()))
_NT = (((1,), (1,)), ((), ()))
_NN = (((1,), (0,)), ((), ()))


def _params(n_grid_dims):
    return pltpu.CompilerParams(
        dimension_semantics=("arbitrary",) * n_grid_dims,
        vmem_limit_bytes=VMEM_LIMIT_BYTES)


_PROJ_CHUNKS = (
    (QA0, 0, NA_WIDTH, NA_WIDTH, True),
    (QB0, 3 * NA_WIDTH, WG_WIDTH, WG_WIDTH, False),
    (KA0, NA_WIDTH, NA_WIDTH, NA_WIDTH, False),
    (VA0, 2 * NA_WIDTH, NA_WIDTH, 0, False),
    (KB0, 3 * NA_WIDTH + WG_WIDTH, 2 * WG_KV_WIDTH, WG_KV_WIDTH, False),
)


def _na_query_order(a, inverse=False):
    chunks = []
    for u in range(a.shape[0] // NA_UNIT):
        for outer in range(NA_UNIT_ROWS if inverse else 2):
            for inner in range(2 if inverse else NA_UNIT_ROWS):
                rr, half = (outer, inner) if inverse else (inner, outer)
                src = (half * NA_UNIT_ROWS + rr) if inverse else (rr * 2 + half)
                r0 = u * NA_UNIT + src * NA_HALF
                chunks.append(a[r0:r0 + NA_HALF])
    return jnp.concatenate(chunks, axis=0)


def _in_proj_kernel(x_ref, g1_ref, w_ref, gcol_ref, o_ref, *, tile):
    n_tiles = x_ref.shape[1] // tile

    def norm(t):
        x = x_ref[0, t * tile:(t + 1) * tile, :]
        ms = jnp.mean(x * x, axis=-1, keepdims=True)
        h = (x * lax.rsqrt(ms + EPS) * g1_ref[...]).astype(jnp.bfloat16)
        return h, (_na_query_order(h) if tile % NA_UNIT == 0 else h)

    def project(hs, chunk):
        _, col0, rows, _, na_order = chunk
        p = jnp.dot(hs[1] if na_order else hs[0], w_ref[:, col0:col0 + rows],
                    preferred_element_type=jnp.float32)
        return p.T

    def head_norm_store(t, chunk, p):
        row0, _, rows, normed_rows, _ = chunk
        ts = slice(t * tile, (t + 1) * tile)
        if normed_rows < rows:
            o_ref[0, row0 + normed_rows:row0 + rows, ts] = p[normed_rows:].astype(jnp.bfloat16)
        for i in range(normed_rows // HEAD_DIM):
            blk = p[i * HEAD_DIM:(i + 1) * HEAD_DIM, :]
            ss = jnp.mean(blk * blk, axis=0, keepdims=True)
            r0 = row0 + i * HEAD_DIM
            y = blk * lax.rsqrt(ss + EPS) * gcol_ref[r0:r0 + HEAD_DIM, :]
            o_ref[0, r0:r0 + HEAD_DIM, ts] = y.astype(jnp.bfloat16)

    items = [(t, chunk) for t in range(n_tiles) for chunk in _PROJ_CHUNKS]
    hs = {0: norm(0)}
    nxt = project(hs[0], items[0][1])
    for idx, (t, chunk) in enumerate(items):
        p = nxt
        if chunk is _PROJ_CHUNKS[-3] and t + 1 < n_tiles:
            hs[t + 1] = norm(t + 1)
        if idx + 1 < len(items):
            t_next, chunk_next = items[idx + 1]
            nxt = project(hs[t_next], chunk_next)
        head_norm_store(t, chunk, p)


def _in_proj(x, g1, w_t, gcol, tm, tile):
    b, n, _ = x.shape
    assert n % tm == 0 and tm % tile == 0
    return pl.pallas_call(
        functools.partial(_in_proj_kernel, tile=tile),
        grid=(b, n // tm),
        in_specs=[
            pl.BlockSpec((1, tm, D_MODEL), lambda i, j: (i, j, 0)),
            pl.BlockSpec((1, D_MODEL), lambda i, j: (0, 0)),
            pl.BlockSpec((D_MODEL, IN_WIDTH), lambda i, j: (0, 0),
                         pipeline_mode=pl.Buffered(1)),
            pl.BlockSpec((IN_WIDTH, 1), lambda i, j: (0, 0)),
        ],
        out_specs=pl.BlockSpec((1, IN_WIDTH, tm), lambda i, j: (i, 0, j)),
        out_shape=jax.ShapeDtypeStruct((b, IN_WIDTH, n), jnp.bfloat16),
        compiler_params=_params(2),
        name="in_proj",
    )(x, g1, w_t, gcol)


def _fold8(x, op):
    acc = x[0:8]
    for r in range(8, x.shape[0], 8):
        acc = op(acc, x[r:r + 8])
    return acc


def _na_window(step, n_units):
    lo = jnp.clip(NA_STEP_UNITS * step - 1, 0, n_units - (NA_STEP_UNITS + 2))
    starts = [jnp.clip(NA_STEP_UNITS * step + x - 1, 0, n_units - NA_KEY_UNITS) - lo
              for x in range(NA_STEP_UNITS)]
    return lo, starts


def _na_kernel(q_ref, kv0_ref, kv1_ref, kv2_ref, kv3_ref, kvm_ref, t0_ref, t1_ref,
               g_ref, o_ref, acc_ref, *, n_units):
    kv_refs = (kv0_ref, kv1_ref, kv2_ref, kv3_ref)
    t_refs = (t0_ref, t1_ref)
    lanes = (slice(0, 2 * GRID_W), slice(2 * GRID_W, 4 * GRID_W))
    _, starts = _na_window(pl.program_id(1), n_units)

    def kv_block(x, c, rows):
        return jnp.where(starts[x] == 1, kv_refs[c + 1][0, rows, :], kv_refs[c][0, rows, :])

    def scores(x, h):
        hs = slice(h * HEAD_DIM, (h + 1) * HEAD_DIM)
        q = q_ref[0, hs, x * NA_UNIT:(x + 1) * NA_UNIT]
        blocks = []
        for c in range(NA_KEY_UNITS):
            s = lax.dot_general(kv_block(x, c, hs), q, _TN,
                                preferred_element_type=jnp.float32)
            for jj in range(NA_UNIT_ROWS):
                j = NA_UNIT_ROWS * c + jj
                pair = []
                for half in range(2):
                    r0 = jj * GRID_W + half * NA_KOFF
                    pair.append(s[r0:r0 + NA_KSPAN, lanes[half]] + t_refs[x][0, h, j, half])
                blocks.append(pair)
        s_meta = lax.dot_general(kvm_ref[hs, :], q, _TN,
                                 preferred_element_type=jnp.float32)
        return blocks, s_meta

    def finish(x, h, blocks, s_meta):
        hs = slice(h * HEAD_DIM, (h + 1) * HEAD_DIM)
        vs = slice(NA_WIDTH + h * HEAD_DIM, NA_WIDTH + (h + 1) * HEAD_DIM)
        m8 = _fold8(s_meta, jnp.maximum)
        m8 = [m8[:, lanes[0]], m8[:, lanes[1]]]
        for pair in blocks:
            for half in range(2):
                m8[half] = jnp.maximum(m8[half], _fold8(pair[half], jnp.maximum))
        m = [jnp.max(m8[half], axis=0, keepdims=True) for half in range(2)]
        p_meta = jnp.exp2(s_meta - jnp.concatenate(m, axis=1))
        l8 = _fold8(p_meta, jnp.add)
        l8 = [l8[:, lanes[0]], l8[:, lanes[1]]]
        o = lax.dot_general(kvm_ref[vs, :], p_meta.astype(jnp.bfloat16), _NN,
                            preferred_element_type=jnp.float32)
        pad = jnp.zeros((NA_KOFF, 2 * GRID_W), jnp.float32)
        for c in range(NA_KEY_UNITS):
            p_rows = []
            for jj in range(NA_UNIT_ROWS):
                pair = blocks[NA_UNIT_ROWS * c + jj]
                p = [jnp.exp2(pair[half] - m[half]) for half in range(2)]
                for half in range(2):
                    l8[half] = l8[half] + _fold8(p[half], jnp.add)
                p_rows.append(jnp.concatenate(
                    [jnp.concatenate([p[0], pad], axis=0),
                     jnp.concatenate([pad, p[1]], axis=0)], axis=1))
            p_c = jnp.concatenate(p_rows, axis=0).astype(jnp.bfloat16)
            o = o + lax.dot_general(kv_block(x, c, vs), p_c, _NN,
                                    preferred_element_type=jnp.float32)
        l = jnp.concatenate([jnp.sum(l8[half], axis=0, keepdims=True) for half in range(2)],
                            axis=1)
        acc_ref[hs, x * NA_UNIT:(x + 1) * NA_UNIT] = o * (1.0 / l)

    items = [(x, h) for x in range(NA_STEP_UNITS) for h in range(NA_HEADS)]
    nxt = scores(*items[0])
    for i, item in enumerate(items):
        cur = nxt
        if i + 1 < len(items):
            nxt = scores(*items[i + 1])
        finish(*item, *cur)

    full = acc_ref[...]
    ms = jnp.mean(full * full, axis=0, keepdims=True)
    o_ref[0] = (full * lax.rsqrt(ms + EPS) * g_ref[...]).astype(jnp.bfloat16)


def _na_slab_index(rows, u):
    n_units = rows // NA_UNIT_ROWS
    k_row0 = NA_UNIT_ROWS * min(max(u - 1, 0), n_units - NA_KEY_UNITS)
    idx = np.full((NA_KEY_UNITS * NA_UNIT_ROWS, NA_UNIT_ROWS), NA_NEG_SLAB, np.int32)
    for j in range(idx.shape[0]):
        for rr in range(NA_UNIT_ROWS):
            kr, r = k_row0 + j, NA_UNIT_ROWS * u + rr
            rs = min(max(r - NA_WIN_ROWS // 2, 0), rows - NA_WIN_ROWS)
            if rs <= kr < rs + NA_WIN_ROWS:
                idx[j, rr] = kr - r + NA_WIN_ROWS - 1
    return idx


def _na_variant(u, n_units):
    return jnp.where(u == 0, 0, jnp.where(u == n_units - 1, 2, 1))


def _na_variants(rows):
    n_units = rows // NA_UNIT_ROWS
    assert rows % NA_UNIT_ROWS == 0 and n_units >= NA_KEY_UNITS
    variants = np.stack([_na_slab_index(rows, u) for u in (0, 1, n_units - 1)])
    for u in range(n_units):
        v = 0 if u == 0 else (2 if u == n_units - 1 else 1)
        assert (_na_slab_index(rows, u) == variants[v]).all()
    return variants


def _na_attn(proj_t, kvm_t, bias, variants, g_col):
    b, _, n = proj_t.shape
    n_units = n // NA_UNIT
    assert (_na_variants(n // GRID_W) == variants).all()
    assert NA_STEP_UNITS == 2 and n_units % NA_STEP_UNITS == 0 and n_units >= NA_STEP_UNITS + 2
    assert VA0 == KA0 + NA_WIDTH and KA0 % (2 * NA_WIDTH) == 0

    def kv_map(c):
        return lambda i, s: (i, KA0 // (2 * NA_WIDTH), _na_window(s, n_units)[0] + c)

    def bias_map(x):
        return lambda i, s: (_na_variant(NA_STEP_UNITS * s + x, n_units), 0, 0, 0, 0, 0)

    blk = (1, NA_WIDTH, NA_STEP_UNITS * NA_UNIT)
    in_specs = [pl.BlockSpec(blk, lambda i, s: (i, QA0 // NA_WIDTH, s))]
    in_specs += [pl.BlockSpec((1, 2 * NA_WIDTH, NA_UNIT), kv_map(c))
                 for c in range(NA_STEP_UNITS + 2)]
    in_specs += [pl.BlockSpec((2 * NA_WIDTH, N_META), lambda i, s: (0, 0))]
    in_specs += [pl.BlockSpec((1,) + bias.shape[1:], bias_map(x)) for x in range(NA_STEP_UNITS)]
    in_specs += [pl.BlockSpec((NA_WIDTH, 1), lambda i, s: (0, 0))]
    return pl.pallas_call(
        functools.partial(_na_kernel, n_units=n_units),
        grid=(b, n_units // NA_STEP_UNITS),
        in_specs=in_specs,
        out_specs=pl.BlockSpec(blk, lambda i, s: (i, 0, s)),
        out_shape=jax.ShapeDtypeStruct((b, NA_WIDTH, n), jnp.bfloat16),
        scratch_shapes=[pltpu.VMEM((NA_WIDTH, NA_STEP_UNITS * NA_UNIT), jnp.float32)],
        compiler_params=_params(2),
        name="na_attn",
    )(proj_t, proj_t, proj_t, proj_t, proj_t, kvm_t, bias, bias, g_col)


def _wg_kernel(q_ref, kvl_ref, kvm_ref, kvr_ref, kvmeta_ref, band_ref, bmeta_ref,
               sink_ref, g_ref, o_ref, acc_ref, *, n_steps):
    j = pl.program_id(1)
    first = j == 0
    last = j == n_steps - 1

    def piece(rows, i):
        if i == 0:
            return kvl_ref[0, rows, :]
        if i == WG_QB + 1:
            return kvr_ref[0, rows, :]
        return kvm_ref[0, rows, (i - 1) * BLOCK:i * BLOCK]

    def scores(g, qb):
        gs = slice(g * HEAD_DIM, (g + 1) * HEAD_DIM)
        q = jnp.concatenate(
            [q_ref[0, (g * WG_GROUP + hh) * HEAD_DIM:(g * WG_GROUP + hh + 1) * HEAD_DIM,
                   qb * BLOCK:(qb + 1) * BLOCK]
             for hh in range(WG_GROUP)], axis=1)
        s_blocks = []
        for c in range(3):
            s = lax.dot_general(piece(gs, qb + c), q, _TN,
                                preferred_element_type=jnp.float32)
            s = s + band_ref[g, c * BLOCK:(c + 1) * BLOCK, :]
            if c == 0 and qb == 0:
                s = s + jnp.where(first, NEG_INF, 0.0)
            if c == 2 and qb == WG_QB - 1:
                s = s + jnp.where(last, NEG_INF, 0.0)
            s_blocks.append(s)
        meta_slab = jnp.where(first, 0, 1) if qb == 0 else 1
        s_meta = lax.dot_general(kvmeta_ref[gs, :], q, _TN,
                                 preferred_element_type=jnp.float32) + bmeta_ref[g, meta_slab]
        return s_blocks, s_meta

    def finish(g, qb, s_blocks, s_meta):
        vs = slice(WG_KV_WIDTH + g * HEAD_DIM, WG_KV_WIDTH + (g + 1) * HEAD_DIM)
        sink = sink_ref[g]
        m8 = _fold8(s_meta, jnp.maximum)
        for s in s_blocks:
            m8 = jnp.maximum(m8, _fold8(s, jnp.maximum))
        m = jnp.maximum(sink, jnp.max(m8, axis=0, keepdims=True))
        p_meta = jnp.exp2(s_meta - m)
        l8 = _fold8(p_meta, jnp.add)
        o = lax.dot_general(kvmeta_ref[vs, :], p_meta.astype(jnp.bfloat16), _NN,
                            preferred_element_type=jnp.float32)
        for c in range(3):
            p = jnp.exp2(s_blocks[c] - m)
            l8 = l8 + _fold8(p, jnp.add)
            o = o + lax.dot_general(piece(vs, qb + c), p.astype(jnp.bfloat16), _NN,
                                    preferred_element_type=jnp.float32)
        l = jnp.exp2(sink - m) + jnp.sum(l8, axis=0, keepdims=True)
        o = o * (1.0 / l)
        for hh in range(WG_GROUP):
            r0 = (g * WG_GROUP + hh) * HEAD_DIM
            acc_ref[r0:r0 + HEAD_DIM, qb * BLOCK:(qb + 1) * BLOCK] = (
                o[:, hh * BLOCK:(hh + 1) * BLOCK])

    units = [(g, qb) for qb in range(WG_QB) for g in range(WG_KV_HEADS)]
    nxt = scores(*units[0])
    for i, unit in enumerate(units):
        cur = nxt
        if i + 1 < len(units):
            nxt = scores(*units[i + 1])
        finish(*unit, *cur)

    full = acc_ref[...]
    ms = jnp.mean(full * full, axis=0, keepdims=True)
    o_ref[0] = (full * lax.rsqrt(ms + EPS) * g_ref[...]).astype(jnp.bfloat16)


def _wg_attn(proj_t, kvm_t, band, bmeta, sink_row, g_col):
    b, _, n = proj_t.shape
    nblk = n // BLOCK
    n_steps = nblk // WG_QB
    assert nblk % WG_QB == 0
    assert VB0 == KB0 + WG_KV_WIDTH and KB0 % (2 * WG_KV_WIDTH) == 0
    kv_rows = KB0 // (2 * WG_KV_WIDTH)

    tq = WG_QB * BLOCK
    side_blk = (1, 2 * WG_KV_WIDTH, BLOCK)
    in_specs = [
        pl.BlockSpec((1, WG_WIDTH, tq), lambda i, j: (i, QB0 // WG_WIDTH, j)),
        pl.BlockSpec(side_blk, lambda i, j: (i, kv_rows, jnp.maximum(j * WG_QB - 1, 0))),
        pl.BlockSpec((1, 2 * WG_KV_WIDTH, tq), lambda i, j: (i, kv_rows, j)),
        pl.BlockSpec(side_blk,
                     lambda i, j: (i, kv_rows, jnp.minimum((j + 1) * WG_QB, nblk - 1))),
        pl.BlockSpec((2 * WG_KV_WIDTH, N_META), lambda i, j: (0, 0)),
        pl.BlockSpec(band.shape, lambda i, j: (0, 0, 0)),
        pl.BlockSpec(bmeta.shape, lambda i, j: (0, 0, 0, 0)),
        pl.BlockSpec(sink_row.shape, lambda i, j: (0, 0, 0)),
        pl.BlockSpec((WG_WIDTH, 1), lambda i, j: (0, 0)),
    ]
    return pl.pallas_call(
        functools.partial(_wg_kernel, n_steps=n_steps),
        grid=(b, n_steps),
        in_specs=in_specs,
        out_specs=pl.BlockSpec((1, WG_WIDTH, tq), lambda i, j: (i, 0, j)),
        out_shape=jax.ShapeDtypeStruct((b, WG_WIDTH, n), jnp.bfloat16),
        scratch_shapes=[pltpu.VMEM((WG_WIDTH, tq), jnp.float32)],
        compiler_params=_params(2),
        name="wg_attn",
    )(proj_t, proj_t, proj_t, proj_t, kvm_t, band, bmeta, sink_row, g_col)


def _out_ffn_kernel(ma_ref, mb_ref, x_ref, woa_ref, wob_ref, g2_ref,
                    wg_ref, wu_ref, wd_ref, o_ref):
    n_tiles = x_ref.shape[1] // FFN_TILE
    n_chunks = D_FF // FFN_FC

    def attn_norm(t):
        ts = slice(t * FFN_TILE, (t + 1) * FFN_TILE)
        attn = lax.dot_general(ma_ref[0, :, ts], woa_ref[...], _TN,
                               preferred_element_type=jnp.float32)
        attn = _na_query_order(attn, inverse=True)
        attn = attn + lax.dot_general(mb_ref[0, :, ts], wob_ref[...], _TN,
                                      preferred_element_type=jnp.float32)
        x1 = x_ref[0, ts, :] + attn
        ms = jnp.mean(x1 * x1, axis=-1, keepdims=True)
        return x1, (x1 * lax.rsqrt(ms + EPS) * g2_ref[...]).astype(jnp.bfloat16)

    def gate_up(h, c):
        cs = slice(c * FFN_FC, (c + 1) * FFN_FC)
        return (jnp.dot(h, wg_ref[:, cs], preferred_element_type=jnp.float32),
                jnp.dot(h, wu_ref[:, cs], preferred_element_type=jnp.float32))

    items = [(t, c) for t in range(n_tiles) for c in range(n_chunks)]
    x1, h, acc = {}, {}, {}
    x1[0], h[0] = attn_norm(0)
    nxt = gate_up(h[0], 0)
    for idx, (t, c) in enumerate(items):
        gate, up = nxt
        if c == n_chunks - FFN_LOOKAHEAD and t + 1 < n_tiles:
            x1[t + 1], h[t + 1] = attn_norm(t + 1)
        if idx + 1 < len(items):
            t_next, c_next = items[idx + 1]
            nxt = gate_up(h[t_next], c_next)
        act = (gate * jax.nn.sigmoid(gate) * up).astype(jnp.bfloat16)
        down = jnp.dot(act, wd_ref[c * FFN_FC:(c + 1) * FFN_FC, :],
                       preferred_element_type=jnp.float32)
        acc[t] = down if c == 0 else acc[t] + down
        if c == n_chunks - 1:
            o_ref[0, t * FFN_TILE:(t + 1) * FFN_TILE, :] = x1[t] + acc[t]


def _out_ffn(mix_a, mix_b, x, wo_a, wo_b, g2, w_gate, w_up, w_down, tm):
    b, n, _ = x.shape

    def const(shape):
        return pl.BlockSpec(shape, lambda i, j: (0,) * len(shape),
                            pipeline_mode=pl.Buffered(1))

    mix_spec = pl.BlockSpec((1, NA_WIDTH, tm), lambda i, j: (i, 0, j))
    x_spec = pl.BlockSpec((1, tm, D_MODEL), lambda i, j: (i, j, 0))
    return pl.pallas_call(
        _out_ffn_kernel,
        grid=(b, n // tm),
        in_specs=[mix_spec, mix_spec, x_spec,
                  const((NA_WIDTH, D_MODEL)), const((WG_WIDTH, D_MODEL)),
                  const((1, D_MODEL)),
                  const((D_MODEL, D_FF)), const((D_MODEL, D_FF)),
                  const((D_FF, D_MODEL))],
        out_specs=x_spec,
        out_shape=jax.ShapeDtypeStruct((b, n, D_MODEL), jnp.float32),
        compiler_params=_params(2),
        name="out_ffn",
    )(mix_a, mix_b, x, wo_a, wo_b, g2, w_gate, w_up, w_down)


def _t5_bucket(rel):
    half = T5_BUCKETS // 2
    max_exact = half // 2
    ret = jnp.where(rel > 0, half, 0)
    n = jnp.abs(rel)
    nf = jnp.maximum(n, 1).astype(jnp.float32)
    large = max_exact + jnp.trunc(jnp.log(nf / max_exact) / math.log(T5_MAX_DIST / max_exact)
                                  * (half - max_exact)).astype(jnp.int32)
    large = jnp.minimum(large, half - 1)
    return ret + jnp.where(n < max_exact, n, large)


def _na_bias_table(rpb, variants):
    n_var, n_j, n_rr = variants.shape
    n_drow, n_dcol = 2 * NA_WIN_ROWS - 1, 2 * NA_WIN_COLS - 1
    sel_row = (variants[..., None] == np.arange(n_drow)).astype(np.float32)
    sel_col = np.zeros((2, NA_KSPAN, NA_HALF, n_dcol), np.float32)
    for half in range(2):
        for k in range(NA_KSPAN):
            for cc in range(NA_HALF):
                kc, c = half * NA_KOFF + k, half * NA_HALF + cc
                cs = min(max(c - NA_WIN_COLS // 2, 0), GRID_W - NA_WIN_COLS)
                if cs <= kc < cs + NA_WIN_COLS:
                    sel_col[half, k, cc, kc - c + NA_WIN_COLS - 1] = 1.0
    sel_col = np.tile(sel_col, (1, 1, n_rr, 1))
    sel_row = np.repeat(sel_row, NA_HALF, axis=2)
    by_col = jnp.einsum("hde,akle->hadkl", rpb.astype(jnp.float32), sel_col,
                        precision=lax.Precision.HIGHEST)
    by_col = jnp.where(sel_col.sum(-1)[None, :, None] > 0, by_col, NEG_INF)
    by_col = jnp.concatenate([by_col, jnp.full_like(by_col[:, :, :1], NEG_INF)], axis=2)
    sel_row = np.concatenate([sel_row, 1.0 - sel_row.sum(-1, keepdims=True)], axis=-1)
    pick = sel_row.transpose(0, 1, 3, 2)[:, None, :, None, :, None, :]
    return jnp.sum(pick * by_col[None, :, None], axis=4)


def _wg_band_bias(t5):
    rel = jnp.arange(-(2 * BLOCK - 1), 2 * BLOCK)
    by_rel = jnp.where((jnp.abs(rel) <= WINDOW)[:, None], t5[_t5_bucket(rel)], NEG_INF)
    rev = by_rel[::-1].T
    tb = jnp.stack([rev[:, 3 * BLOCK - 1 - j:4 * BLOCK - 1 - j] for j in range(3 * BLOCK)],
                   axis=1)
    tb = tb.reshape(WG_KV_HEADS, WG_GROUP, 3 * BLOCK, BLOCK)
    return tb.transpose(0, 2, 1, 3).reshape(WG_KV_HEADS, 3 * BLOCK, WG_GROUP * BLOCK)


def _wg_meta_bias(t5):
    assert BLOCK + 1 >= T5_MAX_DIST
    rel = jnp.arange(N_META)[None, :] - (N_META + jnp.arange(2 * BLOCK)[:, None])
    bm = t5[_t5_bucket(rel)]
    bm = bm.reshape(2, BLOCK, N_META, WG_KV_HEADS, WG_GROUP)
    return bm.transpose(3, 0, 2, 4, 1).reshape(WG_KV_HEADS, 2, N_META, WG_GROUP * BLOCK)


def _encode(x, meta_proj, w_in_t, g1, gcol, na_bias, na_variants, ga_col, band, bmeta, sink_row,
            gb_col, wo_a, wo_b, g2, w_gate, w_up, w_down):
    proj_t = _in_proj(x, g1, w_in_t, gcol, PROJ_TM, PROJ_TILE)
    mix_a = _na_attn(proj_t, meta_proj[KA0:KA0 + 2 * NA_WIDTH], na_bias, na_variants, ga_col)
    mix_b = _wg_attn(proj_t, meta_proj[KB0:KB0 + 2 * WG_KV_WIDTH], band, bmeta, sink_row, gb_col)
    return _out_ffn(mix_a, mix_b, x, wo_a, wo_b, g2, w_gate, w_up, w_down, FFN_TM)


def kernel(x_prompt, x_sample, meta_tokens, t5_table, norm1_g, w_in, qn_a_g, kn_a_g, rpb_a,
           qn_b_g, kn_b_g, sink_b, outn_a_g, outn_b_g, w_out, norm2_g, w_gate, w_up, w_down):
    f32, bf16 = jnp.float32, jnp.bfloat16
    q_scale = HEAD_DIM ** -0.5 * LOG2E
    w_in_t = w_in[0].astype(bf16)
    g1 = norm1_g[0].astype(f32).reshape(1, D_MODEL)
    g2 = norm2_g[0].astype(f32).reshape(1, D_MODEL)
    ones = jnp.ones
    gcol = jnp.concatenate([
        jnp.tile(qn_a_g[0].astype(f32), NA_HEADS) * q_scale,
        jnp.tile(qn_b_g[0].astype(f32), WG_HEADS) * q_scale,
        jnp.tile(kn_a_g[0].astype(f32), NA_HEADS),
        ones((NA_WIDTH,), f32),
        jnp.tile(kn_b_g[0].astype(f32), WG_KV_HEADS),
        ones((WG_KV_WIDTH,), f32),
    ]).reshape(IN_WIDTH, 1)
    ga_col = outn_a_g[0].astype(f32).reshape(NA_WIDTH, 1)
    gb_col = outn_b_g[0].astype(f32).reshape(WG_WIDTH, 1)
    t5 = t5_table.astype(f32) * LOG2E
    na_variants = _na_variants(x_prompt.shape[1] // GRID_W)
    na_bias = _na_bias_table(rpb_a[0].astype(f32) * LOG2E, na_variants)
    band = _wg_band_bias(t5)
    bmeta = _wg_meta_bias(t5)
    sink_row = jnp.repeat(sink_b[0].astype(f32).reshape(WG_KV_HEADS, WG_GROUP) * LOG2E, BLOCK,
                          axis=1).reshape(WG_KV_HEADS, 1, WG_GROUP * BLOCK)
    wo = w_out[0].astype(bf16)
    wo_a, wo_b = wo[:NA_WIDTH], wo[NA_WIDTH:]
    wg, wu, wd = w_gate[0].astype(bf16), w_up[0].astype(bf16), w_down[0].astype(bf16)

    meta = jnp.zeros((1, META_PAD, D_MODEL), f32).at[0, :N_META].set(meta_tokens.astype(f32))
    meta_proj = _in_proj(meta, g1, w_in_t, gcol, META_PAD, META_PAD)[0, :, :N_META]

    def enc(x):
        return _encode(x, meta_proj, w_in_t, g1, gcol, na_bias, na_variants, ga_col, band, bmeta,
                       sink_row, gb_col, wo_a, wo_b, g2, wg, wu, wd)

    return (enc(x_prompt), enc(x_sample))
```

```python
import functools
import math

import jax
import jax.numpy as jnp
import numpy as np
from jax import lax
from jax.experimental import pallas as pl
from jax.experimental.pallas import tpu as pltpu

D_MODEL = 1024
HEAD_DIM = 64
NA_HEADS = 8
WG_HEADS = 8
WG_KV_HEADS = 2
WG_GROUP = WG_HEADS // WG_KV_HEADS
NA_WIDTH = NA_HEADS * HEAD_DIM
WG_WIDTH = WG_HEADS * HEAD_DIM
WG_KV_WIDTH = WG_KV_HEADS * HEAD_DIM
IN_WIDTH = 3 * NA_WIDTH + WG_WIDTH + 2 * WG_KV_WIDTH
D_FF = 2816
GRID_W = 64
NA_WIN_ROWS = 8
NA_WIN_COLS = 16
N_META = 16
WINDOW = 128
BLOCK = 128
T5_BUCKETS = 32
T5_MAX_DIST = 128
EPS = 1e-6
NEG_INF = -1e30

QA0 = 0
QB0 = QA0 + NA_WIDTH
KA0 = QB0 + WG_WIDTH
VA0 = KA0 + NA_WIDTH
KB0 = VA0 + NA_WIDTH
VB0 = KB0 + WG_KV_WIDTH
LOG2E = math.log2(math.e)

NA_UNIT_ROWS = 4
NA_UNIT = NA_UNIT_ROWS * GRID_W
NA_KEY_UNITS = 3
NA_STEP_UNITS = 2
NA_NEG_SLAB = 2 * NA_WIN_ROWS - 1
NA_HALF = GRID_W // 2
NA_KSPAN = NA_HALF + NA_WIN_COLS // 2
NA_KOFF = GRID_W - NA_KSPAN
WG_QB = 8

PROJ_TM = 1024
PROJ_TILE = 256
FFN_TM = 512
FFN_TILE = 256
FFN_FC = 256
FFN_LOOKAHEAD = 4
META_PAD = 128

VMEM_LIMIT_BYTES = 56 * 1024 * 1024

_TN = (((0,), (0,)), ((), ()))
_NT = (((1,), (1,)), ((), ()))
_NN = (((1,), (0,)), ((), ()))


def _params(n_grid_dims):
    return pltpu.CompilerParams(
        dimension_semantics=("arbitrary",) * n_grid_dims,
        vmem_limit_bytes=VMEM_LIMIT_BYTES)


_PROJ_CHUNKS = (
    (QA0, 0, NA_WIDTH, NA_WIDTH, True),
    (QB0, 3 * NA_WIDTH, WG_WIDTH, WG_WIDTH, False),
    (KA0, NA_WIDTH, NA_WIDTH, NA_WIDTH, False),
    (VA0, 2 * NA_WIDTH, NA_WIDTH, 0, False),
    (KB0, 3 * NA_WIDTH + WG_WIDTH, 2 * WG_KV_WIDTH, WG_KV_WIDTH, False),
)


def _na_query_order(a, inverse=False):
    chunks = []
    for u in range(a.shape[0] // NA_UNIT):
        for outer in range(NA_UNIT_ROWS if inverse else 2):
            for inner in range(2 if inverse else NA_UNIT_ROWS):
                rr, half = (outer, inner) if inverse else (inner, outer)
                src = (half * NA_UNIT_ROWS + rr) if inverse else (rr * 2 + half)
                r0 = u * NA_UNIT + src * NA_HALF
                chunks.append(a[r0:r0 + NA_HALF])
    return jnp.concatenate(chunks, axis=0)


def _in_proj_kernel(x_ref, g1_ref, w_ref, gcol_ref, o_ref, *, tile):
    n_tiles = x_ref.shape[1] // tile

    def norm(t):
        x = x_ref[0, t * tile:(t + 1) * tile, :]
        ms = jnp.mean(x * x, axis=-1, keepdims=True)
        h = (x * lax.rsqrt(ms + EPS) * g1_ref[...]).astype(jnp.bfloat16)
        return h, (_na_query_order(h) if tile % NA_UNIT == 0 else h)

    def project(hs, chunk):
        _, col0, rows, _, na_order = chunk
        p = jnp.dot(hs[1] if na_order else hs[0], w_ref[:, col0:col0 + rows],
                    preferred_element_type=jnp.float32)
        return p.T

    def head_norm_store(t, chunk, p):
        row0, _, rows, normed_rows, _ = chunk
        ts = slice(t * tile, (t + 1) * tile)
        if normed_rows < rows:
            o_ref[0, row0 + normed_rows:row0 + rows, ts] = p[normed_rows:].astype(jnp.bfloat16)
        for i in range(normed_rows // HEAD_DIM):
            blk = p[i * HEAD_DIM:(i + 1) * HEAD_DIM, :]
            ss = jnp.mean(blk * blk, axis=0, keepdims=True)
            r0 = row0 + i * HEAD_DIM
            y = blk * lax.rsqrt(ss + EPS) * gcol_ref[r0:r0 + HEAD_DIM, :]
            o_ref[0, r0:r0 + HEAD_DIM, ts] = y.astype(jnp.bfloat16)

    items = [(t, chunk) for t in range(n_tiles) for chunk in _PROJ_CHUNKS]
    hs = {0: norm(0)}
    nxt = project(hs[0], items[0][1])
    for idx, (t, chunk) in enumerate(items):
        p = nxt
        if chunk is _PROJ_CHUNKS[-3] and t + 1 < n_tiles:
            hs[t + 1] = norm(t + 1)
        if idx + 1 < len(items):
            t_next, chunk_next = items[idx + 1]
            nxt = project(hs[t_next], chunk_next)
        head_norm_store(t, chunk, p)


def _in_proj(x, g1, w_t, gcol, tm, tile):
    b, n, _ = x.shape
    assert n % tm == 0 and tm % tile == 0
    return pl.pallas_call(
        functools.partial(_in_proj_kernel, tile=tile),
        grid=(b, n // tm),
        in_specs=[
            pl.BlockSpec((1, tm, D_MODEL), lambda i, j: (i, j, 0)),
            pl.BlockSpec((1, D_MODEL), lambda i, j: (0, 0)),
            pl.BlockSpec((D_MODEL, IN_WIDTH), lambda i, j: (0, 0),
                         pipeline_mode=pl.Buffered(1)),
            pl.BlockSpec((IN_WIDTH, 1), lambda i, j: (0, 0)),
        ],
        out_specs=pl.BlockSpec((1, IN_WIDTH, tm), lambda i, j: (i, 0, j)),
        out_shape=jax.ShapeDtypeStruct((b, IN_WIDTH, n), jnp.bfloat16),
        compiler_params=_params(2),
        name="in_proj",
    )(x, g1, w_t, gcol)


def _fold8(x, op):
    acc = x[0:8]
    for r in range(8, x.shape[0], 8):
        acc = op(acc, x[r:r + 8])
    return acc


def _na_window(step, n_units):
    lo = jnp.clip(NA_STEP_UNITS * step - 1, 0, n_units - (NA_STEP_UNITS + 2))
    starts = [jnp.clip(NA_STEP_UNITS * step + x - 1, 0, n_units - NA_KEY_UNITS) - lo
              for x in range(NA_STEP_UNITS)]
    return lo, starts


def _na_kernel(q_ref, kv0_ref, kv1_ref, kv2_ref, kv3_ref, kvm_ref, t0_ref, t1_ref,
               g_ref, o_ref, acc_ref, *, n_units):
    kv_refs = (kv0_ref, kv1_ref, kv2_ref, kv3_ref)
    t_refs = (t0_ref, t1_ref)
    lanes = (slice(0, 2 * GRID_W), slice(2 * GRID_W, 4 * GRID_W))
    _, starts = _na_window(pl.program_id(1), n_units)

    def kv_block(x, c, rows):
        return jnp.where(starts[x] == 1, kv_refs[c + 1][0, rows, :], kv_refs[c][0, rows, :])

    def scores(x, h):
        hs = slice(h * HEAD_DIM, (h + 1) * HEAD_DIM)
        q = q_ref[0, hs, x * NA_UNIT:(x + 1) * NA_UNIT]
        blocks = []
        for c in range(NA_KEY_UNITS):
            s = lax.dot_general(kv_block(x, c, hs), q, _TN,
                                preferred_element_type=jnp.float32)
            for jj in range(NA_UNIT_ROWS):
                j = NA_UNIT_ROWS * c + jj
                pair = []
                for half in range(2):
                    r0 = jj * GRID_W + half * NA_KOFF
                    pair.append(s[r0:r0 + NA_KSPAN, lanes[half]] + t_refs[x][0, h, j, half])
                blocks.append(pair)
        s_meta = lax.dot_general(kvm_ref[hs, :], q, _TN,
                                 preferred_element_type=jnp.float32)
        return blocks, s_meta

    def finish(x, h, blocks, s_meta):
        hs = slice(h * HEAD_DIM, (h + 1) * HEAD_DIM)
        vs = slice(NA_WIDTH + h * HEAD_DIM, NA_WIDTH + (h + 1) * HEAD_DIM)
        m8 = _fold8(s_meta, jnp.maximum)
        m8 = [m8[:, lanes[0]], m8[:, lanes[1]]]
        for pair in blocks:
            for half in range(2):
                m8[half] = jnp.maximum(m8[half], _fold8(pair[half], jnp.maximum))
        m = [jnp.max(m8[half], axis=0, keepdims=True) for half in range(2)]
        p_meta = jnp.exp2(s_meta - jnp.concatenate(m, axis=1))
        l8 = _fold8(p_meta, jnp.add)
        l8 = [l8[:, lanes[0]], l8[:, lanes[1]]]
        o = lax.dot_general(kvm_ref[vs, :], p_meta.astype(jnp.bfloat16), _NN,
                            preferred_element_type=jnp.float32)
        pad = jnp.zeros((NA_KOFF, 2 * GRID_W), jnp.float32)
        for c in range(NA_KEY_UNITS):
            p_rows = []
            for jj in range(NA_UNIT_ROWS):
                pair = blocks[NA_UNIT_ROWS * c + jj]
                p = [jnp.exp2(pair[half] - m[half]) for half in range(2)]
                for half in range(2):
                    l8[half] = l8[half] + _fold8(p[half], jnp.add)
                p_rows.append(jnp.concatenate(
                    [jnp.concatenate([p[0], pad], axis=0),
                     jnp.concatenate([pad, p[1]], axis=0)], axis=1))
            p_c = jnp.concatenate(p_rows, axis=0).astype(jnp.bfloat16)
            o = o + lax.dot_general(kv_block(x, c, vs), p_c, _NN,
                                    preferred_element_type=jnp.float32)
        l = jnp.concatenate([jnp.sum(l8[half], axis=0, keepdims=True) for half in range(2)],
                            axis=1)
        acc_ref[hs, x * NA_UNIT:(x + 1) * NA_UNIT] = o * (1.0 / l)

    items = [(x, h) for x in range(NA_STEP_UNITS) for h in range(NA_HEADS)]
    nxt = scores(*items[0])
    for i, item in enumerate(items):
        cur = nxt
        if i + 1 < len(items):
            nxt = scores(*items[i + 1])
        finish(*item, *cur)

    full = acc_ref[...]
    ms = jnp.mean(full * full, axis=0, keepdims=True)
    o_ref[0] = (full * lax.rsqrt(ms + EPS) * g_ref[...]).astype(jnp.bfloat16)


def _na_slab_index(rows, u):
    n_units = rows // NA_UNIT_ROWS
    k_row0 = NA_UNIT_ROWS * min(max(u - 1, 0), n_units - NA_KEY_UNITS)
    idx = np.full((NA_KEY_UNITS * NA_UNIT_ROWS, NA_UNIT_ROWS), NA_NEG_SLAB, np.int32)
    for j in range(idx.shape[0]):
        for rr in range(NA_UNIT_ROWS):
            kr, r = k_row0 + j, NA_UNIT_ROWS * u + rr
            rs = min(max(r - NA_WIN_ROWS // 2, 0), rows - NA_WIN_ROWS)
            if rs <= kr < rs + NA_WIN_ROWS:
                idx[j, rr] = kr - r + NA_WIN_ROWS - 1
    return idx


def _na_variant(u, n_units):
    return jnp.where(u == 0, 0, jnp.where(u == n_units - 1, 2, 1))


def _na_variants(rows):
    n_units = rows // NA_UNIT_ROWS
    assert rows % NA_UNIT_ROWS == 0 and n_units >= NA_KEY_UNITS
    variants = np.stack([_na_slab_index(rows, u) for u in (0, 1, n_units - 1)])
    for u in range(n_units):
        v = 0 if u == 0 else (2 if u == n_units - 1 else 1)
        assert (_na_slab_index(rows, u) == variants[v]).all()
    return variants


def _na_attn(proj_t, kvm_t, bias, variants, g_col):
    b, _, n = proj_t.shape
    n_units = n // NA_UNIT
    assert (_na_variants(n // GRID_W) == variants).all()
    assert NA_STEP_UNITS == 2 and n_units % NA_STEP_UNITS == 0 and n_units >= NA_STEP_UNITS + 2
    assert VA0 == KA0 + NA_WIDTH and KA0 % (2 * NA_WIDTH) == 0

    def kv_map(c):
        return lambda i, s: (i, KA0 // (2 * NA_WIDTH), _na_window(s, n_units)[0] + c)

    def bias_map(x):
        return lambda i, s: (_na_variant(NA_STEP_UNITS * s + x, n_units), 0, 0, 0, 0, 0)

    blk = (1, NA_WIDTH, NA_STEP_UNITS * NA_UNIT)
    in_specs = [pl.BlockSpec(blk, lambda i, s: (i, QA0 // NA_WIDTH, s))]
    in_specs += [pl.BlockSpec((1, 2 * NA_WIDTH, NA_UNIT), kv_map(c))
                 for c in range(NA_STEP_UNITS + 2)]
    in_specs += [pl.BlockSpec((2 * NA_WIDTH, N_META), lambda i, s: (0, 0))]
    in_specs += [pl.BlockSpec((1,) + bias.shape[1:], bias_map(x)) for x in range(NA_STEP_UNITS)]
    in_specs += [pl.BlockSpec((NA_WIDTH, 1), lambda i, s: (0, 0))]
    return pl.pallas_call(
        functools.partial(_na_kernel, n_units=n_units),
        grid=(b, n_units // NA_STEP_UNITS),
        in_specs=in_specs,
        out_specs=pl.BlockSpec(blk, lambda i, s: (i, 0, s)),
        out_shape=jax.ShapeDtypeStruct((b, NA_WIDTH, n), jnp.bfloat16),
        scratch_shapes=[pltpu.VMEM((NA_WIDTH, NA_STEP_UNITS * NA_UNIT), jnp.float32)],
        compiler_params=_params(2),
        name="na_attn",
    )(proj_t, proj_t, proj_t, proj_t, proj_t, kvm_t, bias, bias, g_col)


def _wg_kernel(q_ref, kvl_ref, kvm_ref, kvr_ref, kvmeta_ref, band_ref, bmeta_ref,
               sink_ref, g_ref, o_ref, acc_ref, *, n_steps):
    j = pl.program_id(1)
    first = j == 0
    last = j == n_steps - 1

    def piece(rows, i):
        if i == 0:
            return kvl_ref[0, rows, :]
        if i == WG_QB + 1:
            return kvr_ref[0, rows, :]
        return kvm_ref[0, rows, (i - 1) * BLOCK:i * BLOCK]

    def scores(g, qb):
        gs = slice(g * HEAD_DIM, (g + 1) * HEAD_DIM)
        q = jnp.concatenate(
            [q_ref[0, (g * WG_GROUP + hh) * HEAD_DIM:(g * WG_GROUP + hh + 1) * HEAD_DIM,
                   qb * BLOCK:(qb + 1) * BLOCK]
             for hh in range(WG_GROUP)], axis=1)
        s_blocks = []
        for c in range(3):
            s = lax.dot_general(piece(gs, qb + c), q, _TN,
                                preferred_element_type=jnp.float32)
            s = s + band_ref[g, c * BLOCK:(c + 1) * BLOCK, :]
            if c == 0 and qb == 0:
                s = s + jnp.where(first, NEG_INF, 0.0)
            if c == 2 and qb == WG_QB - 1:
                s = s + jnp.where(last, NEG_INF, 0.0)
            s_blocks.append(s)
        meta_slab = jnp.where(first, 0, 1) if qb == 0 else 1
        s_meta = lax.dot_general(kvmeta_ref[gs, :], q, _TN,
                                 preferred_element_type=jnp.float32) + bmeta_ref[g, meta_slab]
        return s_blocks, s_meta

    def finish(g, qb, s_blocks, s_meta):
        vs = slice(WG_KV_WIDTH + g * HEAD_DIM, WG_KV_WIDTH + (g + 1) * HEAD_DIM)
        sink = sink_ref[g]
        m8 = _fold8(s_meta, jnp.maximum)
        for s in s_blocks:
            m8 = jnp.maximum(m8, _fold8(s, jnp.maximum))
        m = jnp.maximum(sink, jnp.max(m8, axis=0, keepdims=True))
        p_meta = jnp.exp2(s_meta - m)
        l8 = _fold8(p_meta, jnp.add)
        o = lax.dot_general(kvmeta_ref[vs, :], p_meta.astype(jnp.bfloat16), _NN,
                            preferred_element_type=jnp.float32)
        for c in range(3):
            p = jnp.exp2(s_blocks[c] - m)
            l8 = l8 + _fold8(p, jnp.add)
            o = o + lax.dot_general(piece(vs, qb + c), p.astype(jnp.bfloat16), _NN,
                                    preferred_element_type=jnp.float32)
        l = jnp.exp2(sink - m) + jnp.sum(l8, axis=0, keepdims=True)
        o = o * (1.0 / l)
        for hh in range(WG_GROUP):
            r0 = (g * WG_GROUP + hh) * HEAD_DIM
            acc_ref[r0:r0 + HEAD_DIM, qb * BLOCK:(qb + 1) * BLOCK] = (
                o[:, hh * BLOCK:(hh + 1) * BLOCK])

    units = [(g, qb) for qb in range(WG_QB) for g in range(WG_KV_HEADS)]
    nxt = scores(*units[0])
    for i, unit in enumerate(units):
        cur = nxt
        if i + 1 < len(units):
            nxt = scores(*units[i + 1])
        finish(*unit, *cur)

    full = acc_ref[...]
    ms = jnp.mean(full * full, axis=0, keepdims=True)
    o_ref[0] = (full * lax.rsqrt(ms + EPS) * g_ref[...]).astype(jnp.bfloat16)


def _wg_attn(proj_t, kvm_t, band, bmeta, sink_row, g_col):
    b, _, n = proj_t.shape
    nblk = n // BLOCK
    n_steps = nblk // WG_QB
    assert nblk % WG_QB == 0
    assert VB0 == KB0 + WG_KV_WIDTH and KB0 % (2 * WG_KV_WIDTH) == 0
    kv_rows = KB0 // (2 * WG_KV_WIDTH)

    tq = WG_QB * BLOCK
    side_blk = (1, 2 * WG_KV_WIDTH, BLOCK)
    in_specs = [
        pl.BlockSpec((1, WG_WIDTH, tq), lambda i, j: (i, QB0 // WG_WIDTH, j)),
        pl.BlockSpec(side_blk, lambda i, j: (i, kv_rows, jnp.maximum(j * WG_QB - 1, 0))),
        pl.BlockSpec((1, 2 * WG_KV_WIDTH, tq), lambda i, j: (i, kv_rows, j)),
        pl.BlockSpec(side_blk,
                     lambda i, j: (i, kv_rows, jnp.minimum((j + 1) * WG_QB, nblk - 1))),
        pl.BlockSpec((2 * WG_KV_WIDTH, N_META), lambda i, j: (0, 0)),
        pl.BlockSpec(band.shape, lambda i, j: (0, 0, 0)),
        pl.BlockSpec(bmeta.shape, lambda i, j: (0, 0, 0, 0)),
        pl.BlockSpec(sink_row.shape, lambda i, j: (0, 0, 0)),
        pl.BlockSpec((WG_WIDTH, 1), lambda i, j: (0, 0)),
    ]
    return pl.pallas_call(
        functools.partial(_wg_kernel, n_steps=n_steps),
        grid=(b, n_steps),
        in_specs=in_specs,
        out_specs=pl.BlockSpec((1, WG_WIDTH, tq), lambda i, j: (i, 0, j)),
        out_shape=jax.ShapeDtypeStruct((b, WG_WIDTH, n), jnp.bfloat16),
        scratch_shapes=[pltpu.VMEM((WG_WIDTH, tq), jnp.float32)],
        compiler_params=_params(2),
        name="wg_attn",
    )(proj_t, proj_t, proj_t, proj_t, kvm_t, band, bmeta, sink_row, g_col)


def _out_ffn_kernel(ma_ref, mb_ref, x_ref, woa_ref, wob_ref, g2_ref,
                    wg_ref, wu_ref, wd_ref, o_ref):
    n_tiles = x_ref.shape[1] // FFN_TILE
    n_chunks = D_FF // FFN_FC

    def attn_norm(t):
        ts = slice(t * FFN_TILE, (t + 1) * FFN_TILE)
        attn = lax.dot_general(ma_ref[0, :, ts], woa_ref[...], _TN,
                               preferred_element_type=jnp.float32)
        attn = _na_query_order(attn, inverse=True)
        attn = attn + lax.dot_general(mb_ref[0, :, ts], wob_ref[...], _TN,
                                      preferred_element_type=jnp.float32)
        x1 = x_ref[0, ts, :] + attn
        ms = jnp.mean(x1 * x1, axis=-1, keepdims=True)
        return x1, (x1 * lax.rsqrt(ms + EPS) * g2_ref[...]).astype(jnp.bfloat16)

    def gate_up(h, c):
        cs = slice(c * FFN_FC, (c + 1) * FFN_FC)
        return (jnp.dot(h, wg_ref[:, cs], preferred_element_type=jnp.float32),
                jnp.dot(h, wu_ref[:, cs], preferred_element_type=jnp.float32))

    items = [(t, c) for t in range(n_tiles) for c in range(n_chunks)]
    x1, h, acc = {}, {}, {}
    x1[0], h[0] = attn_norm(0)
    nxt = gate_up(h[0], 0)
    for idx, (t, c) in enumerate(items):
        gate, up = nxt
        if c == n_chunks - FFN_LOOKAHEAD and t + 1 < n_tiles:
            x1[t + 1], h[t + 1] = attn_norm(t + 1)
        if idx + 1 < len(items):
            t_next, c_next = items[idx + 1]
            nxt = gate_up(h[t_next], c_next)
        act = (gate * jax.nn.sigmoid(gate) * up).astype(jnp.bfloat16)
        down = jnp.dot(act, wd_ref[c * FFN_FC:(c + 1) * FFN_FC, :],
                       preferred_element_type=jnp.float32)
        acc[t] = down if c == 0 else acc[t] + down
        if c == n_chunks - 1:
            o_ref[0, t * FFN_TILE:(t + 1) * FFN_TILE, :] = x1[t] + acc[t]


def _out_ffn(mix_a, mix_b, x, wo_a, wo_b, g2, w_gate, w_up, w_down, tm):
    b, n, _ = x.shape

    def const(shape):
        return pl.BlockSpec(shape, lambda i, j: (0,) * len(shape),
                            pipeline_mode=pl.Buffered(1))

    mix_spec = pl.BlockSpec((1, NA_WIDTH, tm), lambda i, j: (i, 0, j))
    x_spec = pl.BlockSpec((1, tm, D_MODEL), lambda i, j: (i, j, 0))
    return pl.pallas_call(
        _out_ffn_kernel,
        grid=(b, n // tm),
        in_specs=[mix_spec, mix_spec, x_spec,
                  const((NA_WIDTH, D_MODEL)), const((WG_WIDTH, D_MODEL)),
                  const((1, D_MODEL)),
                  const((D_MODEL, D_FF)), const((D_MODEL, D_FF)),
                  const((D_FF, D_MODEL))],
        out_specs=x_spec,
        out_shape=jax.ShapeDtypeStruct((b, n, D_MODEL), jnp.float32),
        compiler_params=_params(2),
        name="out_ffn",
    )(mix_a, mix_b, x, wo_a, wo_b, g2, w_gate, w_up, w_down)


def _t5_bucket(rel):
    half = T5_BUCKETS // 2
    max_exact = half // 2
    ret = jnp.where(rel > 0, half, 0)
    n = jnp.abs(rel)
    nf = jnp.maximum(n, 1).astype(jnp.float32)
    large = max_exact + jnp.trunc(jnp.log(nf / max_exact) / math.log(T5_MAX_DIST / max_exact)
                                  * (half - max_exact)).astype(jnp.int32)
    large = jnp.minimum(large, half - 1)
    return ret + jnp.where(n < max_exact, n, large)


def _na_bias_table(rpb, variants):
    n_var, n_j, n_rr = variants.shape
    n_drow, n_dcol = 2 * NA_WIN_ROWS - 1, 2 * NA_WIN_COLS - 1
    sel_row = (variants[..., None] == np.arange(n_drow)).astype(np.float32)
    sel_col = np.zeros((2, NA_KSPAN, NA_HALF, n_dcol), np.float32)
    for half in range(2):
        for k in range(NA_KSPAN):
            for cc in range(NA_HALF):
                kc, c = half * NA_KOFF + k, half * NA_HALF + cc
                cs = min(max(c - NA_WIN_COLS // 2, 0), GRID_W - NA_WIN_COLS)
                if cs <= kc < cs + NA_WIN_COLS:
                    sel_col[half, k, cc, kc - c + NA_WIN_COLS - 1] = 1.0
    sel_col = np.tile(sel_col, (1, 1, n_rr, 1))
    sel_row = np.repeat(sel_row, NA_HALF, axis=2)
    by_col = jnp.einsum("hde,akle->hadkl", rpb.astype(jnp.float32), sel_col,
                        precision=lax.Precision.HIGHEST)
    by_col = jnp.where(sel_col.sum(-1)[None, :, None] > 0, by_col, NEG_INF)
    by_col = jnp.concatenate([by_col, jnp.full_like(by_col[:, :, :1], NEG_INF)], axis=2)
    sel_row = np.concatenate([sel_row, 1.0 - sel_row.sum(-1, keepdims=True)], axis=-1)
    pick = sel_row.transpose(0, 1, 3, 2)[:, None, :, None, :, None, :]
    return jnp.sum(pick * by_col[None, :, None], axis=4)


_T5_LANES = 4 * BLOCK


def _t5_layout_kernel(band_vec_ref, meta_vec_ref, band_ref, bmeta_ref):
    for h in range(WG_HEADS):
        g, hh = divmod(h, WG_GROUP)
        lanes = slice(hh * BLOCK, (hh + 1) * BLOCK)
        x = jnp.broadcast_to(band_vec_ref[h:h + 1, :], (3 * BLOCK, _T5_LANES))
        x = pltpu.roll(x, _T5_LANES - (3 * BLOCK - 1), 1, stride=1, stride_axis=0)
        band_ref[g, :, lanes] = x[:, :BLOCK]
        y = jnp.broadcast_to(meta_vec_ref[h:h + 1, :], (N_META, _T5_LANES))
        y = pltpu.roll(y, _T5_LANES - (N_META - 1), 1, stride=1, stride_axis=0)
        for s in range(2):
            bmeta_ref[g, s, :, lanes] = y[:, s * BLOCK:(s + 1) * BLOCK]


def _wg_bias_tables(t5):
    assert BLOCK + 1 >= T5_MAX_DIST
    rel = 2 * BLOCK - 1 - jnp.arange(_T5_LANES)
    band_vec = jnp.where((jnp.abs(rel) <= WINDOW)[:, None], t5[_t5_bucket(rel)], NEG_INF)
    meta_vec = t5[_t5_bucket(-1 - jnp.arange(_T5_LANES))]
    return pl.pallas_call(
        _t5_layout_kernel,
        out_shape=(
            jax.ShapeDtypeStruct((WG_KV_HEADS, 3 * BLOCK, WG_GROUP * BLOCK), jnp.float32),
            jax.ShapeDtypeStruct((WG_KV_HEADS, 2, N_META, WG_GROUP * BLOCK), jnp.float32)),
        name="t5_layout",
    )(band_vec.T, meta_vec.T)


def _encode(x, meta_proj, w_in_t, g1, gcol, na_bias, na_variants, ga_col, band, bmeta, sink_row,
            gb_col, wo_a, wo_b, g2, w_gate, w_up, w_down):
    proj_t = _in_proj(x, g1, w_in_t, gcol, PROJ_TM, PROJ_TILE)
    mix_a = _na_attn(proj_t, meta_proj[KA0:KA0 + 2 * NA_WIDTH], na_bias, na_variants, ga_col)
    mix_b = _wg_attn(proj_t, meta_proj[KB0:KB0 + 2 * WG_KV_WIDTH], band, bmeta, sink_row, gb_col)
    return _out_ffn(mix_a, mix_b, x, wo_a, wo_b, g2, w_gate, w_up, w_down, FFN_TM)


def kernel(x_prompt, x_sample, meta_tokens, t5_table, norm1_g, w_in, qn_a_g, kn_a_g, rpb_a,
           qn_b_g, kn_b_g, sink_b, outn_a_g, outn_b_g, w_out, norm2_g, w_gate, w_up, w_down):
    f32, bf16 = jnp.float32, jnp.bfloat16
    q_scale = HEAD_DIM ** -0.5 * LOG2E
    w_in_t = w_in[0].astype(bf16)
    g1 = norm1_g[0].astype(f32).reshape(1, D_MODEL)
    g2 = norm2_g[0].astype(f32).reshape(1, D_MODEL)
    ones = jnp.ones
    gcol = jnp.concatenate([
        jnp.tile(qn_a_g[0].astype(f32), NA_HEADS) * q_scale,
        jnp.tile(qn_b_g[0].astype(f32), WG_HEADS) * q_scale,
        jnp.tile(kn_a_g[0].astype(f32), NA_HEADS),
        ones((NA_WIDTH,), f32),
        jnp.tile(kn_b_g[0].astype(f32), WG_KV_HEADS),
        ones((WG_KV_WIDTH,), f32),
    ]).reshape(IN_WIDTH, 1)
    ga_col = outn_a_g[0].astype(f32).reshape(NA_WIDTH, 1)
    gb_col = outn_b_g[0].astype(f32).reshape(WG_WIDTH, 1)
    t5 = t5_table.astype(f32) * LOG2E
    na_variants = _na_variants(x_prompt.shape[1] // GRID_W)
    na_bias = _na_bias_table(rpb_a[0].astype(f32) * LOG2E, na_variants)
    band, bmeta = _wg_bias_tables(t5)
    sink_row = jnp.repeat(sink_b[0].astype(f32).reshape(WG_KV_HEADS, WG_GROUP) * LOG2E, BLOCK,
                          axis=1).reshape(WG_KV_HEADS, 1, WG_GROUP * BLOCK)
    wo = w_out[0].astype(bf16)
    wo_a, wo_b = wo[:NA_WIDTH], wo[NA_WIDTH:]
    wg, wu, wd = w_gate[0].astype(bf16), w_up[0].astype(bf16), w_down[0].astype(bf16)

    meta = jnp.zeros((1, META_PAD, D_MODEL), f32).at[0, :N_META].set(meta_tokens.astype(f32))
    meta_proj = _in_proj(meta, g1, w_in_t, gcol, META_PAD, META_PAD)[0, :, :N_META]

    def enc(x):
        return _encode(x, meta_proj, w_in_t, g1, gcol, na_bias, na_variants, ga_col, band, bmeta,
                       sink_row, gb_col, wo_a, wo_b, g2, wg, wu, wd)

    return (enc(x_prompt), enc(x_sample))
```

```python
import functools
import math

import jax
import jax.numpy as jnp
import numpy as np
from jax import lax
from jax.experimental import pallas as pl
from jax.experimental.pallas import tpu as pltpu

D_MODEL = 1024
HEAD_DIM = 64
NA_HEADS = 8
WG_HEADS = 8
WG_KV_HEADS = 2
WG_GROUP = WG_HEADS // WG_KV_HEADS
NA_WIDTH = NA_HEADS * HEAD_DIM
WG_WIDTH = WG_HEADS * HEAD_DIM
WG_KV_WIDTH = WG_KV_HEADS * HEAD_DIM
IN_WIDTH = 3 * NA_WIDTH + WG_WIDTH + 2 * WG_KV_WIDTH
D_FF = 2816
GRID_W = 64
NA_WIN_ROWS = 8
NA_WIN_COLS = 16
N_META = 16
WINDOW = 128
BLOCK = 128
T5_BUCKETS = 32
T5_MAX_DIST = 128
EPS = 1e-6
NEG_INF = -1e30

QA0 = 0
QB0 = QA0 + NA_WIDTH
KA0 = QB0 + WG_WIDTH
VA0 = KA0 + NA_WIDTH
KB0 = VA0 + NA_WIDTH
VB0 = KB0 + WG_KV_WIDTH
LOG2E = math.log2(math.e)

NA_UNIT_ROWS = 4
NA_UNIT = NA_UNIT_ROWS * GRID_W
NA_KEY_UNITS = 3
NA_STEP_UNITS = 2
NA_NEG_SLAB = 2 * NA_WIN_ROWS - 1
NA_HALF = GRID_W // 2
NA_KSPAN = NA_HALF + NA_WIN_COLS // 2
NA_KOFF = GRID_W - NA_KSPAN
WG_QB = 8

PROJ_TM = 2048
PROJ_TILE = 256
FFN_TM = 1024
FFN_TILE = 256
FFN_FC = 256
FFN_LOOKAHEAD = 4
META_PAD = 128

VMEM_LIMIT_BYTES = 56 * 1024 * 1024

_TN = (((0,), (0,)), ((), ()))
_NT = (((1,), (1,)), ((), ()))
_NN = (((1,), (0,)), ((), ()))


def _params(n_grid_dims):
    return pltpu.CompilerParams(
        dimension_semantics=("arbitrary",) * n_grid_dims,
        vmem_limit_bytes=VMEM_LIMIT_BYTES)


_PROJ_CHUNKS = (
    (QA0, 0, NA_WIDTH, NA_WIDTH, True),
    (QB0, 3 * NA_WIDTH, WG_WIDTH, WG_WIDTH, False),
    (KA0, NA_WIDTH, NA_WIDTH, NA_WIDTH, False),
    (VA0, 2 * NA_WIDTH, NA_WIDTH, 0, False),
    (KB0, 3 * NA_WIDTH + WG_WIDTH, 2 * WG_KV_WIDTH, WG_KV_WIDTH, False),
)


def _na_query_order(a, inverse=False):
    chunks = []
    for u in range(a.shape[0] // NA_UNIT):
        for outer in range(NA_UNIT_ROWS if inverse else 2):
            for inner in range(2 if inverse else NA_UNIT_ROWS):
                rr, half = (outer, inner) if inverse else (inner, outer)
                src = (half * NA_UNIT_ROWS + rr) if inverse else (rr * 2 + half)
                r0 = u * NA_UNIT + src * NA_HALF
                chunks.append(a[r0:r0 + NA_HALF])
    return jnp.concatenate(chunks, axis=0)


def _in_proj_kernel(x_ref, g1_ref, w_ref, gcol_ref, o_ref, *, tile):
    n_tiles = x_ref.shape[1] // tile

    def norm(t):
        x = x_ref[0, t * tile:(t + 1) * tile, :]
        ms = jnp.mean(x * x, axis=-1, keepdims=True)
        h = (x * lax.rsqrt(ms + EPS) * g1_ref[...]).astype(jnp.bfloat16)
        return h, (_na_query_order(h) if tile % NA_UNIT == 0 else h)

    def project(hs, chunk):
        _, col0, rows, _, na_order = chunk
        p = jnp.dot(hs[1] if na_order else hs[0], w_ref[:, col0:col0 + rows],
                    preferred_element_type=jnp.float32)
        return p.T

    def head_norm_store(t, chunk, p):
        row0, _, rows, normed_rows, _ = chunk
        ts = slice(t * tile, (t + 1) * tile)
        if normed_rows < rows:
            o_ref[0, row0 + normed_rows:row0 + rows, ts] = p[normed_rows:].astype(jnp.bfloat16)
        for i in range(normed_rows // HEAD_DIM):
            blk = p[i * HEAD_DIM:(i + 1) * HEAD_DIM, :]
            ss = jnp.mean(blk * blk, axis=0, keepdims=True)
            r0 = row0 + i * HEAD_DIM
            y = blk * lax.rsqrt(ss + EPS) * gcol_ref[r0:r0 + HEAD_DIM, :]
            o_ref[0, r0:r0 + HEAD_DIM, ts] = y.astype(jnp.bfloat16)

    items = [(t, chunk) for t in range(n_tiles) for chunk in _PROJ_CHUNKS]
    hs = {0: norm(0)}
    nxt = project(hs[0], items[0][1])
    for idx, (t, chunk) in enumerate(items):
        p = nxt
        if chunk is _PROJ_CHUNKS[-3] and t + 1 < n_tiles:
            hs[t + 1] = norm(t + 1)
        if idx + 1 < len(items):
            t_next, chunk_next = items[idx + 1]
            nxt = project(hs[t_next], chunk_next)
        head_norm_store(t, chunk, p)


def _in_proj(x, g1, w_t, gcol, tm, tile):
    b, n, _ = x.shape
    assert n % tm == 0 and tm % tile == 0
    return pl.pallas_call(
        functools.partial(_in_proj_kernel, tile=tile),
        grid=(b, n // tm),
        in_specs=[
            pl.BlockSpec((1, tm, D_MODEL), lambda i, j: (i, j, 0)),
            pl.BlockSpec((1, D_MODEL), lambda i, j: (0, 0)),
            pl.BlockSpec((D_MODEL, IN_WIDTH), lambda i, j: (0, 0),
                         pipeline_mode=pl.Buffered(1)),
            pl.BlockSpec((IN_WIDTH, 1), lambda i, j: (0, 0)),
        ],
        out_specs=pl.BlockSpec((1, IN_WIDTH, tm), lambda i, j: (i, 0, j)),
        out_shape=jax.ShapeDtypeStruct((b, IN_WIDTH, n), jnp.bfloat16),
        compiler_params=_params(2),
        name="in_proj",
    )(x, g1, w_t, gcol)


def _fold8(x, op):
    acc = x[0:8]
    for r in range(8, x.shape[0], 8):
        acc = op(acc, x[r:r + 8])
    return acc


def _na_window(step, n_units):
    lo = jnp.clip(NA_STEP_UNITS * step - 1, 0, n_units - (NA_STEP_UNITS + 2))
    starts = [jnp.clip(NA_STEP_UNITS * step + x - 1, 0, n_units - NA_KEY_UNITS) - lo
              for x in range(NA_STEP_UNITS)]
    return lo, starts


def _na_kernel(q_ref, kv0_ref, kv1_ref, kv2_ref, kv3_ref, kvm_ref, t0_ref, t1_ref,
               g_ref, o_ref, acc_ref, *, n_units):
    kv_refs = (kv0_ref, kv1_ref, kv2_ref, kv3_ref)
    t_refs = (t0_ref, t1_ref)
    lanes = (slice(0, 2 * GRID_W), slice(2 * GRID_W, 4 * GRID_W))
    _, starts = _na_window(pl.program_id(1), n_units)

    def kv_block(x, c, rows):
        return jnp.where(starts[x] == 1, kv_refs[c + 1][0, rows, :], kv_refs[c][0, rows, :])

    def scores(x, h):
        hs = slice(h * HEAD_DIM, (h + 1) * HEAD_DIM)
        q = q_ref[0, hs, x * NA_UNIT:(x + 1) * NA_UNIT]
        blocks = []
        for c in range(NA_KEY_UNITS):
            s = lax.dot_general(kv_block(x, c, hs), q, _TN,
                                preferred_element_type=jnp.float32)
            for jj in range(NA_UNIT_ROWS):
                j = NA_UNIT_ROWS * c + jj
                pair = []
                for half in range(2):
                    r0 = jj * GRID_W + half * NA_KOFF
                    pair.append(s[r0:r0 + NA_KSPAN, lanes[half]] + t_refs[x][0, h, j, half])
                blocks.append(pair)
        s_meta = lax.dot_general(kvm_ref[hs, :], q, _TN,
                                 preferred_element_type=jnp.float32)
        return blocks, s_meta

    def finish(x, h, blocks, s_meta):
        hs = slice(h * HEAD_DIM, (h + 1) * HEAD_DIM)
        vs = slice(NA_WIDTH + h * HEAD_DIM, NA_WIDTH + (h + 1) * HEAD_DIM)
        m8 = _fold8(s_meta, jnp.maximum)
        m8 = [m8[:, lanes[0]], m8[:, lanes[1]]]
        for pair in blocks:
            for half in range(2):
                m8[half] = jnp.maximum(m8[half], _fold8(pair[half], jnp.maximum))
        m = [jnp.max(m8[half], axis=0, keepdims=True) for half in range(2)]
        p_meta = jnp.exp2(s_meta - jnp.concatenate(m, axis=1))
        l8 = _fold8(p_meta, jnp.add)
        l8 = [l8[:, lanes[0]], l8[:, lanes[1]]]
        o = lax.dot_general(kvm_ref[vs, :], p_meta.astype(jnp.bfloat16), _NN,
                            preferred_element_type=jnp.float32)
        pad = jnp.zeros((NA_KOFF, 2 * GRID_W), jnp.float32)
        for c in range(NA_KEY_UNITS):
            p_rows = []
            for jj in range(NA_UNIT_ROWS):
                pair = blocks[NA_UNIT_ROWS * c + jj]
                p = [jnp.exp2(pair[half] - m[half]) for half in range(2)]
                for half in range(2):
                    l8[half] = l8[half] + _fold8(p[half], jnp.add)
                p_rows.append(jnp.concatenate(
                    [jnp.concatenate([p[0], pad], axis=0),
                     jnp.concatenate([pad, p[1]], axis=0)], axis=1))
            p_c = jnp.concatenate(p_rows, axis=0).astype(jnp.bfloat16)
            o = o + lax.dot_general(kv_block(x, c, vs), p_c, _NN,
                                    preferred_element_type=jnp.float32)
        l = jnp.concatenate([jnp.sum(l8[half], axis=0, keepdims=True) for half in range(2)],
                            axis=1)
        acc_ref[hs, x * NA_UNIT:(x + 1) * NA_UNIT] = o * (1.0 / l)

    items = [(x, h) for x in range(NA_STEP_UNITS) for h in range(NA_HEADS)]
    nxt = scores(*items[0])
    for i, item in enumerate(items):
        cur = nxt
        if i + 1 < len(items):
            nxt = scores(*items[i + 1])
        finish(*item, *cur)

    full = acc_ref[...]
    ms = jnp.mean(full * full, axis=0, keepdims=True)
    o_ref[0] = (full * lax.rsqrt(ms + EPS) * g_ref[...]).astype(jnp.bfloat16)


def _na_slab_index(rows, u):
    n_units = rows // NA_UNIT_ROWS
    k_row0 = NA_UNIT_ROWS * min(max(u - 1, 0), n_units - NA_KEY_UNITS)
    idx = np.full((NA_KEY_UNITS * NA_UNIT_ROWS, NA_UNIT_ROWS), NA_NEG_SLAB, np.int32)
    for j in range(idx.shape[0]):
        for rr in range(NA_UNIT_ROWS):
            kr, r = k_row0 + j, NA_UNIT_ROWS * u + rr
            rs = min(max(r - NA_WIN_ROWS // 2, 0), rows - NA_WIN_ROWS)
            if rs <= kr < rs + NA_WIN_ROWS:
                idx[j, rr] = kr - r + NA_WIN_ROWS - 1
    return idx


def _na_variant(u, n_units):
    return jnp.where(u == 0, 0, jnp.where(u == n_units - 1, 2, 1))


def _na_variants(rows):
    n_units = rows // NA_UNIT_ROWS
    assert rows % NA_UNIT_ROWS == 0 and n_units >= NA_KEY_UNITS
    variants = np.stack([_na_slab_index(rows, u) for u in (0, 1, n_units - 1)])
    for u in range(n_units):
        v = 0 if u == 0 else (2 if u == n_units - 1 else 1)
        assert (_na_slab_index(rows, u) == variants[v]).all()
    return variants


def _na_attn(proj_t, kvm_t, bias, variants, g_col):
    b, _, n = proj_t.shape
    n_units = n // NA_UNIT
    assert (_na_variants(n // GRID_W) == variants).all()
    assert NA_STEP_UNITS == 2 and n_units % NA_STEP_UNITS == 0 and n_units >= NA_STEP_UNITS + 2
    assert VA0 == KA0 + NA_WIDTH and KA0 % (2 * NA_WIDTH) == 0

    def kv_map(c):
        return lambda i, s: (i, KA0 // (2 * NA_WIDTH), _na_window(s, n_units)[0] + c)

    def bias_map(x):
        return lambda i, s: (_na_variant(NA_STEP_UNITS * s + x, n_units), 0, 0, 0, 0, 0)

    blk = (1, NA_WIDTH, NA_STEP_UNITS * NA_UNIT)
    in_specs = [pl.BlockSpec(blk, lambda i, s: (i, QA0 // NA_WIDTH, s))]
    in_specs += [pl.BlockSpec((1, 2 * NA_WIDTH, NA_UNIT), kv_map(c))
                 for c in range(NA_STEP_UNITS + 2)]
    in_specs += [pl.BlockSpec((2 * NA_WIDTH, N_META), lambda i, s: (0, 0))]
    in_specs += [pl.BlockSpec((1,) + bias.shape[1:], bias_map(x)) for x in range(NA_STEP_UNITS)]
    in_specs += [pl.BlockSpec((NA_WIDTH, 1), lambda i, s: (0, 0))]
    return pl.pallas_call(
        functools.partial(_na_kernel, n_units=n_units),
        grid=(b, n_units // NA_STEP_UNITS),
        in_specs=in_specs,
        out_specs=pl.BlockSpec(blk, lambda i, s: (i, 0, s)),
        out_shape=jax.ShapeDtypeStruct((b, NA_WIDTH, n), jnp.bfloat16),
        scratch_shapes=[pltpu.VMEM((NA_WIDTH, NA_STEP_UNITS * NA_UNIT), jnp.float32)],
        compiler_params=_params(2),
        name="na_attn",
    )(proj_t, proj_t, proj_t, proj_t, proj_t, kvm_t, bias, bias, g_col)


def _wg_kernel(q_ref, kvl_ref, kvm_ref, kvr_ref, kvmeta_ref, band_ref, bmeta_ref,
               sink_ref, g_ref, o_ref, acc_ref, *, n_steps):
    j = pl.program_id(1)
    first = j == 0
    last = j == n_steps - 1

    def piece(rows, i):
        if i == 0:
            return kvl_ref[0, rows, :]
        if i == WG_QB + 1:
            return kvr_ref[0, rows, :]
        return kvm_ref[0, rows, (i - 1) * BLOCK:i * BLOCK]

    def scores(g, qb):
        gs = slice(g * HEAD_DIM, (g + 1) * HEAD_DIM)
        q = jnp.concatenate(
            [q_ref[0, (g * WG_GROUP + hh) * HEAD_DIM:(g * WG_GROUP + hh + 1) * HEAD_DIM,
                   qb * BLOCK:(qb + 1) * BLOCK]
             for hh in range(WG_GROUP)], axis=1)
        s_blocks = []
        for c in range(3):
            s = lax.dot_general(piece(gs, qb + c), q, _TN,
                                preferred_element_type=jnp.float32)
            s = s + band_ref[g, c * BLOCK:(c + 1) * BLOCK, :]
            if c == 0 and qb == 0:
                s = s + jnp.where(first, NEG_INF, 0.0)
            if c == 2 and qb == WG_QB - 1:
                s = s + jnp.where(last, NEG_INF, 0.0)
            s_blocks.append(s)
        meta_slab = jnp.where(first, 0, 1) if qb == 0 else 1
        s_meta = lax.dot_general(kvmeta_ref[gs, :], q, _TN,
                                 preferred_element_type=jnp.float32) + bmeta_ref[g, meta_slab]
        return s_blocks, s_meta

    def finish(g, qb, s_blocks, s_meta):
        vs = slice(WG_KV_WIDTH + g * HEAD_DIM, WG_KV_WIDTH + (g + 1) * HEAD_DIM)
        sink = sink_ref[g]
        m8 = _fold8(s_meta, jnp.maximum)
        for s in s_blocks:
            m8 = jnp.maximum(m8, _fold8(s, jnp.maximum))
        m = jnp.maximum(sink, jnp.max(m8, axis=0, keepdims=True))
        p_meta = jnp.exp2(s_meta - m)
        l8 = _fold8(p_meta, jnp.add)
        o = lax.dot_general(kvmeta_ref[vs, :], p_meta.astype(jnp.bfloat16), _NN,
                            preferred_element_type=jnp.float32)
        for c in range(3):
            p = jnp.exp2(s_blocks[c] - m)
            l8 = l8 + _fold8(p, jnp.add)
            o = o + lax.dot_general(piece(vs, qb + c), p.astype(jnp.bfloat16), _NN,
                                    preferred_element_type=jnp.float32)
        l = jnp.exp2(sink - m) + jnp.sum(l8, axis=0, keepdims=True)
        o = o * (1.0 / l)
        for hh in range(WG_GROUP):
            r0 = (g * WG_GROUP + hh) * HEAD_DIM
            acc_ref[r0:r0 + HEAD_DIM, qb * BLOCK:(qb + 1) * BLOCK] = (
                o[:, hh * BLOCK:(hh + 1) * BLOCK])

    units = [(g, qb) for qb in range(WG_QB) for g in range(WG_KV_HEADS)]
    nxt = scores(*units[0])
    for i, unit in enumerate(units):
        cur = nxt
        if i + 1 < len(units):
            nxt = scores(*units[i + 1])
        finish(*unit, *cur)

    full = acc_ref[...]
    ms = jnp.mean(full * full, axis=0, keepdims=True)
    o_ref[0] = (full * lax.rsqrt(ms + EPS) * g_ref[...]).astype(jnp.bfloat16)


def _wg_attn(proj_t, kvm_t, band, bmeta, sink_row, g_col):
    b, _, n = proj_t.shape
    nblk = n // BLOCK
    n_steps = nblk // WG_QB
    assert nblk % WG_QB == 0
    assert VB0 == KB0 + WG_KV_WIDTH and KB0 % (2 * WG_KV_WIDTH) == 0
    kv_rows = KB0 // (2 * WG_KV_WIDTH)

    tq = WG_QB * BLOCK
    side_blk = (1, 2 * WG_KV_WIDTH, BLOCK)
    in_specs = [
        pl.BlockSpec((1, WG_WIDTH, tq), lambda i, j: (i, QB0 // WG_WIDTH, j)),
        pl.BlockSpec(side_blk, lambda i, j: (i, kv_rows, jnp.maximum(j * WG_QB - 1, 0))),
        pl.BlockSpec((1, 2 * WG_KV_WIDTH, tq), lambda i, j: (i, kv_rows, j)),
        pl.BlockSpec(side_blk,
                     lambda i, j: (i, kv_rows, jnp.minimum((j + 1) * WG_QB, nblk - 1))),
        pl.BlockSpec((2 * WG_KV_WIDTH, N_META), lambda i, j: (0, 0)),
        pl.BlockSpec(band.shape, lambda i, j: (0, 0, 0)),
        pl.BlockSpec(bmeta.shape, lambda i, j: (0, 0, 0, 0)),
        pl.BlockSpec(sink_row.shape, lambda i, j: (0, 0, 0)),
        pl.BlockSpec((WG_WIDTH, 1), lambda i, j: (0, 0)),
    ]
    return pl.pallas_call(
        functools.partial(_wg_kernel, n_steps=n_steps),
        grid=(b, n_steps),
        in_specs=in_specs,
        out_specs=pl.BlockSpec((1, WG_WIDTH, tq), lambda i, j: (i, 0, j)),
        out_shape=jax.ShapeDtypeStruct((b, WG_WIDTH, n), jnp.bfloat16),
        scratch_shapes=[pltpu.VMEM((WG_WIDTH, tq), jnp.float32)],
        compiler_params=_params(2),
        name="wg_attn",
    )(proj_t, proj_t, proj_t, proj_t, kvm_t, band, bmeta, sink_row, g_col)


def _out_ffn_kernel(ma_ref, mb_ref, x_ref, woa_ref, wob_ref, g2_ref,
                    wg_ref, wu_ref, wd_ref, o_ref):
    n_tiles = x_ref.shape[1] // FFN_TILE
    n_chunks = D_FF // FFN_FC

    def attn_norm(t):
        ts = slice(t * FFN_TILE, (t + 1) * FFN_TILE)
        attn = lax.dot_general(ma_ref[0, :, ts], woa_ref[...], _TN,
                               preferred_element_type=jnp.float32)
        attn = _na_query_order(attn, inverse=True)
        attn = attn + lax.dot_general(mb_ref[0, :, ts], wob_ref[...], _TN,
                                      preferred_element_type=jnp.float32)
        x1 = x_ref[0, ts, :] + attn
        ms = jnp.mean(x1 * x1, axis=-1, keepdims=True)
        return x1, (x1 * lax.rsqrt(ms + EPS) * g2_ref[...]).astype(jnp.bfloat16)

    def gate_up(h, c):
        cs = slice(c * FFN_FC, (c + 1) * FFN_FC)
        return (jnp.dot(h, wg_ref[:, cs], preferred_element_type=jnp.float32),
                jnp.dot(h, wu_ref[:, cs], preferred_element_type=jnp.float32))

    items = [(t, c) for t in range(n_tiles) for c in range(n_chunks)]
    x1, h, acc = {}, {}, {}
    x1[0], h[0] = attn_norm(0)
    nxt = gate_up(h[0], 0)
    for idx, (t, c) in enumerate(items):
        gate, up = nxt
        if c == n_chunks - FFN_LOOKAHEAD and t + 1 < n_tiles:
            x1[t + 1], h[t + 1] = attn_norm(t + 1)
        if idx + 1 < len(items):
            t_next, c_next = items[idx + 1]
            nxt = gate_up(h[t_next], c_next)
        act = (gate * jax.nn.sigmoid(gate) * up).astype(jnp.bfloat16)
        down = jnp.dot(act, wd_ref[c * FFN_FC:(c + 1) * FFN_FC, :],
                       preferred_element_type=jnp.float32)
        acc[t] = down if c == 0 else acc[t] + down
        if c == n_chunks - 1:
            o_ref[0, t * FFN_TILE:(t + 1) * FFN_TILE, :] = x1[t] + acc[t]


def _out_ffn(mix_a, mix_b, x, wo_a, wo_b, g2, w_gate, w_up, w_down, tm):
    b, n, _ = x.shape

    def const(shape):
        return pl.BlockSpec(shape, lambda i, j: (0,) * len(shape),
                            pipeline_mode=pl.Buffered(1))

    mix_spec = pl.BlockSpec((1, NA_WIDTH, tm), lambda i, j: (i, 0, j))
    x_spec = pl.BlockSpec((1, tm, D_MODEL), lambda i, j: (i, j, 0))
    return pl.pallas_call(
        _out_ffn_kernel,
        grid=(b, n // tm),
        in_specs=[mix_spec, mix_spec, x_spec,
                  const((NA_WIDTH, D_MODEL)), const((WG_WIDTH, D_MODEL)),
                  const((1, D_MODEL)),
                  const((D_MODEL, D_FF)), const((D_MODEL, D_FF)),
                  const((D_FF, D_MODEL))],
        out_specs=x_spec,
        out_shape=jax.ShapeDtypeStruct((b, n, D_MODEL), jnp.float32),
        compiler_params=_params(2),
        name="out_ffn",
    )(mix_a, mix_b, x, wo_a, wo_b, g2, w_gate, w_up, w_down)


def _t5_bucket(rel):
    half = T5_BUCKETS // 2
    max_exact = half // 2
    ret = jnp.where(rel > 0, half, 0)
    n = jnp.abs(rel)
    nf = jnp.maximum(n, 1).astype(jnp.float32)
    large = max_exact + jnp.trunc(jnp.log(nf / max_exact) / math.log(T5_MAX_DIST / max_exact)
                                  * (half - max_exact)).astype(jnp.int32)
    large = jnp.minimum(large, half - 1)
    return ret + jnp.where(n < max_exact, n, large)


def _na_bias_table(rpb, variants):
    n_var, n_j, n_rr = variants.shape
    n_dcol = 2 * NA_WIN_COLS - 1
    sel_col = np.zeros((2, NA_KSPAN, NA_HALF, n_dcol), np.float32)
    for half in range(2):
        for k in range(NA_KSPAN):
            for cc in range(NA_HALF):
                kc, c = half * NA_KOFF + k, half * NA_HALF + cc
                cs = min(max(c - NA_WIN_COLS // 2, 0), GRID_W - NA_WIN_COLS)
                if cs <= kc < cs + NA_WIN_COLS:
                    sel_col[half, k, cc, kc - c + NA_WIN_COLS - 1] = 1.0
    sel_col = np.tile(sel_col, (1, 1, n_rr, 1))
    by_col = jnp.einsum("hde,akle->hadkl", rpb.astype(jnp.float32), sel_col,
                        precision=lax.Precision.HIGHEST)
    by_col = jnp.where(sel_col.sum(-1)[None, :, None] > 0, by_col, NEG_INF)
    by_col = jnp.concatenate([by_col, jnp.full_like(by_col[:, :, :1], NEG_INF)], axis=2)
    return pl.pallas_call(
        functools.partial(_na_table_kernel, variants=variants),
        grid=(NA_HEADS,),
        in_specs=[pl.BlockSpec((1,) + by_col.shape[1:], lambda h: (h, 0, 0, 0, 0))],
        out_specs=pl.BlockSpec((n_var, 1, n_j, 2, NA_KSPAN, n_rr * NA_HALF),
                               lambda h: (0, h, 0, 0, 0, 0)),
        out_shape=jax.ShapeDtypeStruct((n_var, NA_HEADS, n_j, 2, NA_KSPAN, n_rr * NA_HALF),
                                       jnp.float32),
        name="na_table",
    )(by_col)


def _na_table_kernel(by_col_ref, o_ref, *, variants):
    n_var, n_j, n_rr = variants.shape
    lane = lax.broadcasted_iota(jnp.int32, (NA_KSPAN, n_rr * NA_HALF), 1)
    for v in range(n_var):
        for j in range(n_j):
            for a in range(2):
                slab = by_col_ref[0, a, int(variants[v, j, n_rr - 1])]
                for rr in reversed(range(n_rr - 1)):
                    slab = jnp.where(lane < (rr + 1) * NA_HALF,
                                     by_col_ref[0, a, int(variants[v, j, rr])], slab)
                o_ref[v, 0, j, a] = slab


_T5_LANES = 4 * BLOCK


def _t5_layout_kernel(band_vec_ref, meta_vec_ref, band_ref, bmeta_ref):
    for h in range(WG_HEADS):
        g, hh = divmod(h, WG_GROUP)
        lanes = slice(hh * BLOCK, (hh + 1) * BLOCK)
        x = jnp.broadcast_to(band_vec_ref[h:h + 1, :], (3 * BLOCK, _T5_LANES))
        x = pltpu.roll(x, _T5_LANES - (3 * BLOCK - 1), 1, stride=1, stride_axis=0)
        band_ref[g, :, lanes] = x[:, :BLOCK]
        y = jnp.broadcast_to(meta_vec_ref[h:h + 1, :], (N_META, _T5_LANES))
        y = pltpu.roll(y, _T5_LANES - (N_META - 1), 1, stride=1, stride_axis=0)
        for s in range(2):
            bmeta_ref[g, s, :, lanes] = y[:, s * BLOCK:(s + 1) * BLOCK]


def _wg_bias_tables(t5):
    assert BLOCK + 1 >= T5_MAX_DIST
    rel = 2 * BLOCK - 1 - jnp.arange(_T5_LANES)
    band_vec = jnp.where((jnp.abs(rel) <= WINDOW)[:, None], t5[_t5_bucket(rel)], NEG_INF)
    meta_vec = t5[_t5_bucket(-1 - jnp.arange(_T5_LANES))]
    return pl.pallas_call(
        _t5_layout_kernel,
        out_shape=(
            jax.ShapeDtypeStruct((WG_KV_HEADS, 3 * BLOCK, WG_GROUP * BLOCK), jnp.float32),
            jax.ShapeDtypeStruct((WG_KV_HEADS, 2, N_META, WG_GROUP * BLOCK), jnp.float32)),
        name="t5_layout",
    )(band_vec.T, meta_vec.T)


def _encode(x, meta_proj, w_in_t, g1, gcol, na_bias, na_variants, ga_col, band, bmeta, sink_row,
            gb_col, wo_a, wo_b, g2, w_gate, w_up, w_down):
    proj_t = _in_proj(x, g1, w_in_t, gcol, PROJ_TM, PROJ_TILE)
    mix_a = _na_attn(proj_t, meta_proj[KA0:KA0 + 2 * NA_WIDTH], na_bias, na_variants, ga_col)
    mix_b = _wg_attn(proj_t, meta_proj[KB0:KB0 + 2 * WG_KV_WIDTH], band, bmeta, sink_row, gb_col)
    return _out_ffn(mix_a, mix_b, x, wo_a, wo_b, g2, w_gate, w_up, w_down, FFN_TM)


def kernel(x_prompt, x_sample, meta_tokens, t5_table, norm1_g, w_in, qn_a_g, kn_a_g, rpb_a,
           qn_b_g, kn_b_g, sink_b, outn_a_g, outn_b_g, w_out, norm2_g, w_gate, w_up, w_down):
    f32, bf16 = jnp.float32, jnp.bfloat16
    q_scale = HEAD_DIM ** -0.5 * LOG2E
    w_in_t = w_in[0].astype(bf16)
    g1 = norm1_g[0].astype(f32).reshape(1, D_MODEL)
    g2 = norm2_g[0].astype(f32).reshape(1, D_MODEL)
    ones = jnp.ones
    gcol = jnp.concatenate([
        jnp.tile(qn_a_g[0].astype(f32), NA_HEADS) * q_scale,
        jnp.tile(qn_b_g[0].astype(f32), WG_HEADS) * q_scale,
        jnp.tile(kn_a_g[0].astype(f32), NA_HEADS),
        ones((NA_WIDTH,), f32),
        jnp.tile(kn_b_g[0].astype(f32), WG_KV_HEADS),
        ones((WG_KV_WIDTH,), f32),
    ]).reshape(IN_WIDTH, 1)
    ga_col = outn_a_g[0].astype(f32).reshape(NA_WIDTH, 1)
    gb_col = outn_b_g[0].astype(f32).reshape(WG_WIDTH, 1)
    t5 = t5_table.astype(f32) * LOG2E
    na_variants = _na_variants(x_prompt.shape[1] // GRID_W)
    na_bias = _na_bias_table(rpb_a[0].astype(f32) * LOG2E, na_variants)
    band, bmeta = _wg_bias_tables(t5)
    sink_row = jnp.repeat(sink_b[0].astype(f32).reshape(WG_KV_HEADS, WG_GROUP) * LOG2E, BLOCK,
                          axis=1).reshape(WG_KV_HEADS, 1, WG_GROUP * BLOCK)
    wo = w_out[0].astype(bf16)
    wo_a, wo_b = wo[:NA_WIDTH], wo[NA_WIDTH:]
    wg, wu, wd = w_gate[0].astype(bf16), w_up[0].astype(bf16), w_down[0].astype(bf16)

    meta = jnp.zeros((1, META_PAD, D_MODEL), f32).at[0, :N_META].set(meta_tokens.astype(f32))
    meta_proj = _in_proj(meta, g1, w_in_t, gcol, META_PAD, META_PAD)[0, :, :N_META]

    def enc(x):
        return _encode(x, meta_proj, w_in_t, g1, gcol, na_bias, na_variants, ga_col, band, bmeta,
                       sink_row, gb_col, wo_a, wo_b, g2, wg, wu, wd)

    return (enc(x_prompt), enc(x_sample))
```

```python
import functools
import math

import jax
import jax.numpy as jnp
import numpy as np
from jax import lax
from jax.experimental import pallas as pl
from jax.experimental.pallas import tpu as pltpu

D_MODEL = 1024
HEAD_DIM = 64
NA_HEADS = 8
WG_HEADS = 8
WG_KV_HEADS = 2
WG_GROUP = WG_HEADS // WG_KV_HEADS
NA_WIDTH = NA_HEADS * HEAD_DIM
WG_WIDTH = WG_HEADS * HEAD_DIM
WG_KV_WIDTH = WG_KV_HEADS * HEAD_DIM
IN_WIDTH = 3 * NA_WIDTH + WG_WIDTH + 2 * WG_KV_WIDTH
D_FF = 2816
GRID_W = 64
NA_WIN_ROWS = 8
NA_WIN_COLS = 16
N_META = 16
WINDOW = 128
BLOCK = 128
T5_BUCKETS = 32
T5_MAX_DIST = 128
EPS = 1e-6
NEG_INF = -1e30

QA0 = 0
QB0 = QA0 + NA_WIDTH
KA0 = QB0 + WG_WIDTH
VA0 = KA0 + NA_WIDTH
KB0 = VA0 + NA_WIDTH
VB0 = KB0 + WG_KV_WIDTH
LOG2E = math.log2(math.e)

NA_UNIT_ROWS = 4
NA_UNIT = NA_UNIT_ROWS * GRID_W
NA_KEY_UNITS = 3
NA_STEP_UNITS = 2
NA_NEG_SLAB = 2 * NA_WIN_ROWS - 1
NA_HALF = GRID_W // 2
NA_KSPAN = NA_HALF + NA_WIN_COLS // 2
NA_KOFF = GRID_W - NA_KSPAN
WG_QB = 8

PROJ_TM = 2048
PROJ_TILE = 256
FFN_TM = 1024
FFN_TILE = 256
FFN_FC = 256
FFN_LOOKAHEAD = 4
META_PAD = 128

VMEM_LIMIT_BYTES = 56 * 1024 * 1024

_TN = (((0,), (0,)), ((), ()))
_NT = (((1,), (1,)), ((), ()))
_NN = (((1,), (0,)), ((), ()))


def _params(n_grid_dims):
    return pltpu.CompilerParams(
        dimension_semantics=("arbitrary",) * n_grid_dims,
        vmem_limit_bytes=VMEM_LIMIT_BYTES)


_PROJ_CHUNKS = (
    (QA0, 0, NA_WIDTH, NA_WIDTH, True),
    (QB0, 3 * NA_WIDTH, WG_WIDTH, WG_WIDTH, False),
    (KA0, NA_WIDTH, NA_WIDTH, NA_WIDTH, False),
    (VA0, 2 * NA_WIDTH, NA_WIDTH, 0, False),
    (KB0, 3 * NA_WIDTH + WG_WIDTH, 2 * WG_KV_WIDTH, WG_KV_WIDTH, False),
)


def _na_query_order(a, inverse=False):
    chunks = []
    for u in range(a.shape[0] // NA_UNIT):
        for outer in range(NA_UNIT_ROWS if inverse else 2):
            for inner in range(2 if inverse else NA_UNIT_ROWS):
                rr, half = (outer, inner) if inverse else (inner, outer)
                src = (half * NA_UNIT_ROWS + rr) if inverse else (rr * 2 + half)
                r0 = u * NA_UNIT + src * NA_HALF
                chunks.append(a[r0:r0 + NA_HALF])
    return jnp.concatenate(chunks, axis=0)


def _in_proj_kernel(x_ref, g1_ref, w_ref, gcol_ref, o_ref, *, tile):
    n_tiles = x_ref.shape[1] // tile

    def norm(t):
        x = x_ref[0, t * tile:(t + 1) * tile, :]
        ms = jnp.mean(x * x, axis=-1, keepdims=True)
        h = (x * lax.rsqrt(ms + EPS) * g1_ref[...]).astype(jnp.bfloat16)
        return h, (_na_query_order(h) if tile % NA_UNIT == 0 else h)

    def project(hs, chunk):
        _, col0, rows, _, na_order = chunk
        p = jnp.dot(hs[1] if na_order else hs[0], w_ref[:, col0:col0 + rows],
                    preferred_element_type=jnp.float32)
        return p.T

    def head_norm_store(t, chunk, p):
        row0, _, rows, normed_rows, _ = chunk
        ts = slice(t * tile, (t + 1) * tile)
        if normed_rows < rows:
            o_ref[0, row0 + normed_rows:row0 + rows, ts] = p[normed_rows:].astype(jnp.bfloat16)
        for i in range(normed_rows // HEAD_DIM):
            blk = p[i * HEAD_DIM:(i + 1) * HEAD_DIM, :]
            ss = jnp.mean(blk * blk, axis=0, keepdims=True)
            r0 = row0 + i * HEAD_DIM
            y = blk * lax.rsqrt(ss + EPS) * gcol_ref[r0:r0 + HEAD_DIM, :]
            o_ref[0, r0:r0 + HEAD_DIM, ts] = y.astype(jnp.bfloat16)

    items = [(t, chunk) for t in range(n_tiles) for chunk in _PROJ_CHUNKS]
    hs = {0: norm(0)}
    nxt = project(hs[0], items[0][1])
    for idx, (t, chunk) in enumerate(items):
        p = nxt
        if chunk is _PROJ_CHUNKS[-3] and t + 1 < n_tiles:
            hs[t + 1] = norm(t + 1)
        if idx + 1 < len(items):
            t_next, chunk_next = items[idx + 1]
            nxt = project(hs[t_next], chunk_next)
        head_norm_store(t, chunk, p)


def _in_proj(x, g1, w_t, gcol, tm, tile):
    b, n, _ = x.shape
    assert n % tm == 0 and tm % tile == 0
    return pl.pallas_call(
        functools.partial(_in_proj_kernel, tile=tile),
        grid=(b, n // tm),
        in_specs=[
            pl.BlockSpec((1, tm, D_MODEL), lambda i, j: (i, j, 0)),
            pl.BlockSpec((1, D_MODEL), lambda i, j: (0, 0)),
            pl.BlockSpec((D_MODEL, IN_WIDTH), lambda i, j: (0, 0),
                         pipeline_mode=pl.Buffered(1)),
            pl.BlockSpec((IN_WIDTH, 1), lambda i, j: (0, 0)),
        ],
        out_specs=pl.BlockSpec((1, IN_WIDTH, tm), lambda i, j: (i, 0, j)),
        out_shape=jax.ShapeDtypeStruct((b, IN_WIDTH, n), jnp.bfloat16),
        compiler_params=_params(2),
        name="in_proj",
    )(x, g1, w_t, gcol)


def _fold8(x, op):
    acc = x[0:8]
    for r in range(8, x.shape[0], 8):
        acc = op(acc, x[r:r + 8])
    return acc


def _na_window(step, n_units):
    lo = jnp.clip(NA_STEP_UNITS * step - 1, 0, n_units - (NA_STEP_UNITS + 2))
    starts = [jnp.clip(NA_STEP_UNITS * step + x - 1, 0, n_units - NA_KEY_UNITS) - lo
              for x in range(NA_STEP_UNITS)]
    return lo, starts


def _na_kernel(q_ref, kv0_ref, kv1_ref, kv2_ref, kv3_ref, kvm_ref, t0_ref, t1_ref,
               g_ref, o_ref, acc_ref, *, n_units):
    kv_refs = (kv0_ref, kv1_ref, kv2_ref, kv3_ref)
    t_refs = (t0_ref, t1_ref)
    lanes = (slice(0, 2 * GRID_W), slice(2 * GRID_W, 4 * GRID_W))
    _, starts = _na_window(pl.program_id(1), n_units)

    def kv_block(x, c, rows):
        return jnp.where(starts[x] == 1, kv_refs[c + 1][0, rows, :], kv_refs[c][0, rows, :])

    def scores(x, h):
        hs = slice(h * HEAD_DIM, (h + 1) * HEAD_DIM)
        q = q_ref[0, hs, x * NA_UNIT:(x + 1) * NA_UNIT]
        blocks = []
        for c in range(NA_KEY_UNITS):
            s = lax.dot_general(kv_block(x, c, hs), q, _TN,
                                preferred_element_type=jnp.float32)
            for jj in range(NA_UNIT_ROWS):
                j = NA_UNIT_ROWS * c + jj
                pair = []
                for half in range(2):
                    r0 = jj * GRID_W + half * NA_KOFF
                    pair.append(s[r0:r0 + NA_KSPAN, lanes[half]] + t_refs[x][0, h, j, half])
                blocks.append(pair)
        s_meta = lax.dot_general(kvm_ref[hs, :], q, _TN,
                                 preferred_element_type=jnp.float32)
        return blocks, s_meta

    def finish(x, h, blocks, s_meta):
        hs = slice(h * HEAD_DIM, (h + 1) * HEAD_DIM)
        vs = slice(NA_WIDTH + h * HEAD_DIM, NA_WIDTH + (h + 1) * HEAD_DIM)
        m8 = _fold8(s_meta, jnp.maximum)
        m8 = [m8[:, lanes[0]], m8[:, lanes[1]]]
        for pair in blocks:
            for half in range(2):
                m8[half] = jnp.maximum(m8[half], _fold8(pair[half], jnp.maximum))
        m = [jnp.max(m8[half], axis=0, keepdims=True) for half in range(2)]
        p_meta = jnp.exp2(s_meta - jnp.concatenate(m, axis=1))
        l8 = _fold8(p_meta, jnp.add)
        l8 = [l8[:, lanes[0]], l8[:, lanes[1]]]
        pad = jnp.zeros((NA_KOFF, 2 * GRID_W), jnp.float32)
        p_rows, values = [], []
        for c in range(NA_KEY_UNITS):
            values.append(kv_block(x, c, vs))
            for jj in range(NA_UNIT_ROWS):
                pair = blocks[NA_UNIT_ROWS * c + jj]
                p = [jnp.exp2(pair[half] - m[half]) for half in range(2)]
                for half in range(2):
                    l8[half] = l8[half] + _fold8(p[half], jnp.add)
                p_rows.append(jnp.concatenate(
                    [jnp.concatenate([p[0], pad], axis=0),
                     jnp.concatenate([pad, p[1]], axis=0)], axis=1))
        o = lax.dot_general(
            jnp.concatenate(values + [kvm_ref[vs, :]], axis=1),
            jnp.concatenate(p_rows + [p_meta], axis=0).astype(jnp.bfloat16),
            _NN, preferred_element_type=jnp.float32)
        l = jnp.concatenate([jnp.sum(l8[half], axis=0, keepdims=True) for half in range(2)],
                            axis=1)
        acc_ref[hs, x * NA_UNIT:(x + 1) * NA_UNIT] = o * (1.0 / l)

    items = [(x, h) for x in range(NA_STEP_UNITS) for h in range(NA_HEADS)]
    nxt = scores(*items[0])
    for i, item in enumerate(items):
        cur = nxt
        if i + 1 < len(items):
            nxt = scores(*items[i + 1])
        finish(*item, *cur)

    full = acc_ref[...]
    ms = jnp.mean(full * full, axis=0, keepdims=True)
    o_ref[0] = (full * lax.rsqrt(ms + EPS) * g_ref[...]).astype(jnp.bfloat16)


def _na_slab_index(rows, u):
    n_units = rows // NA_UNIT_ROWS
    k_row0 = NA_UNIT_ROWS * min(max(u - 1, 0), n_units - NA_KEY_UNITS)
    idx = np.full((NA_KEY_UNITS * NA_UNIT_ROWS, NA_UNIT_ROWS), NA_NEG_SLAB, np.int32)
    for j in range(idx.shape[0]):
        for rr in range(NA_UNIT_ROWS):
            kr, r = k_row0 + j, NA_UNIT_ROWS * u + rr
            rs = min(max(r - NA_WIN_ROWS // 2, 0), rows - NA_WIN_ROWS)
            if rs <= kr < rs + NA_WIN_ROWS:
                idx[j, rr] = kr - r + NA_WIN_ROWS - 1
    return idx


def _na_variant(u, n_units):
    return jnp.where(u == 0, 0, jnp.where(u == n_units - 1, 2, 1))


def _na_variants(rows):
    n_units = rows // NA_UNIT_ROWS
    assert rows % NA_UNIT_ROWS == 0 and n_units >= NA_KEY_UNITS
    variants = np.stack([_na_slab_index(rows, u) for u in (0, 1, n_units - 1)])
    for u in range(n_units):
        v = 0 if u == 0 else (2 if u == n_units - 1 else 1)
        assert (_na_slab_index(rows, u) == variants[v]).all()
    return variants


def _na_attn(proj_t, kvm_t, bias, variants, g_col):
    b, _, n = proj_t.shape
    n_units = n // NA_UNIT
    assert (_na_variants(n // GRID_W) == variants).all()
    assert NA_STEP_UNITS == 2 and n_units % NA_STEP_UNITS == 0 and n_units >= NA_STEP_UNITS + 2
    assert VA0 == KA0 + NA_WIDTH and KA0 % (2 * NA_WIDTH) == 0

    def kv_map(c):
        return lambda i, s: (i, KA0 // (2 * NA_WIDTH), _na_window(s, n_units)[0] + c)

    def bias_map(x):
        return lambda i, s: (_na_variant(NA_STEP_UNITS * s + x, n_units), 0, 0, 0, 0, 0)

    blk = (1, NA_WIDTH, NA_STEP_UNITS * NA_UNIT)
    in_specs = [pl.BlockSpec(blk, lambda i, s: (i, QA0 // NA_WIDTH, s))]
    in_specs += [pl.BlockSpec((1, 2 * NA_WIDTH, NA_UNIT), kv_map(c))
                 for c in range(NA_STEP_UNITS + 2)]
    in_specs += [pl.BlockSpec((2 * NA_WIDTH, N_META), lambda i, s: (0, 0))]
    in_specs += [pl.BlockSpec((1,) + bias.shape[1:], bias_map(x)) for x in range(NA_STEP_UNITS)]
    in_specs += [pl.BlockSpec((NA_WIDTH, 1), lambda i, s: (0, 0))]
    return pl.pallas_call(
        functools.partial(_na_kernel, n_units=n_units),
        grid=(b, n_units // NA_STEP_UNITS),
        in_specs=in_specs,
        out_specs=pl.BlockSpec(blk, lambda i, s: (i, 0, s)),
        out_shape=jax.ShapeDtypeStruct((b, NA_WIDTH, n), jnp.bfloat16),
        scratch_shapes=[pltpu.VMEM((NA_WIDTH, NA_STEP_UNITS * NA_UNIT), jnp.float32)],
        compiler_params=_params(2),
        name="na_attn",
    )(proj_t, proj_t, proj_t, proj_t, proj_t, kvm_t, bias, bias, g_col)


def _wg_kernel(q_ref, kvl_ref, kvm_ref, kvr_ref, kvmeta_ref, band_ref, bmeta_ref,
               sink_ref, g_ref, o_ref, acc_ref, *, n_steps):
    j = pl.program_id(1)
    first = j == 0
    last = j == n_steps - 1

    def piece(rows, i):
        if i == 0:
            return kvl_ref[0, rows, :]
        if i == WG_QB + 1:
            return kvr_ref[0, rows, :]
        return kvm_ref[0, rows, (i - 1) * BLOCK:i * BLOCK]

    def scores(g, qb):
        gs = slice(g * HEAD_DIM, (g + 1) * HEAD_DIM)
        q = jnp.concatenate(
            [q_ref[0, (g * WG_GROUP + hh) * HEAD_DIM:(g * WG_GROUP + hh + 1) * HEAD_DIM,
                   qb * BLOCK:(qb + 1) * BLOCK]
             for hh in range(WG_GROUP)], axis=1)
        s_blocks = []
        for c in range(3):
            s = lax.dot_general(piece(gs, qb + c), q, _TN,
                                preferred_element_type=jnp.float32)
            s = s + band_ref[g, c * BLOCK:(c + 1) * BLOCK, :]
            if c == 0 and qb == 0:
                s = s + jnp.where(first, NEG_INF, 0.0)
            if c == 2 and qb == WG_QB - 1:
                s = s + jnp.where(last, NEG_INF, 0.0)
            s_blocks.append(s)
        meta_slab = jnp.where(first, 0, 1) if qb == 0 else 1
        s_meta = lax.dot_general(kvmeta_ref[gs, :], q, _TN,
                                 preferred_element_type=jnp.float32) + bmeta_ref[g, meta_slab]
        return s_blocks, s_meta

    def finish(g, qb, s_blocks, s_meta):
        vs = slice(WG_KV_WIDTH + g * HEAD_DIM, WG_KV_WIDTH + (g + 1) * HEAD_DIM)
        sink = sink_ref[g]
        m8 = _fold8(s_meta, jnp.maximum)
        for s in s_blocks:
            m8 = jnp.maximum(m8, _fold8(s, jnp.maximum))
        m = jnp.maximum(sink, jnp.max(m8, axis=0, keepdims=True))
        p_meta = jnp.exp2(s_meta - m)
        l8 = _fold8(p_meta, jnp.add)
        probs, values = [], []
        for c in range(3):
            p = jnp.exp2(s_blocks[c] - m)
            l8 = l8 + _fold8(p, jnp.add)
            probs.append(p.astype(jnp.bfloat16))
            values.append(piece(vs, qb + c))
        o = lax.dot_general(
            jnp.concatenate(values + [kvmeta_ref[vs, :]], axis=1),
            jnp.concatenate(probs + [p_meta.astype(jnp.bfloat16)], axis=0),
            _NN, preferred_element_type=jnp.float32)
        l = jnp.exp2(sink - m) + jnp.sum(l8, axis=0, keepdims=True)
        o = o * (1.0 / l)
        for hh in range(WG_GROUP):
            r0 = (g * WG_GROUP + hh) * HEAD_DIM
            acc_ref[r0:r0 + HEAD_DIM, qb * BLOCK:(qb + 1) * BLOCK] = (
                o[:, hh * BLOCK:(hh + 1) * BLOCK])

    units = [(g, qb) for qb in range(WG_QB) for g in range(WG_KV_HEADS)]
    nxt = scores(*units[0])
    for i, unit in enumerate(units):
        cur = nxt
        if i + 1 < len(units):
            nxt = scores(*units[i + 1])
        finish(*unit, *cur)

    full = acc_ref[...]
    ms = jnp.mean(full * full, axis=0, keepdims=True)
    o_ref[0] = (full * lax.rsqrt(ms + EPS) * g_ref[...]).astype(jnp.bfloat16)


def _wg_attn(proj_t, kvm_t, band, bmeta, sink_row, g_col):
    b, _, n = proj_t.shape
    nblk = n // BLOCK
    n_steps = nblk // WG_QB
    assert nblk % WG_QB == 0
    assert VB0 == KB0 + WG_KV_WIDTH and KB0 % (2 * WG_KV_WIDTH) == 0
    kv_rows = KB0 // (2 * WG_KV_WIDTH)

    tq = WG_QB * BLOCK
    side_blk = (1, 2 * WG_KV_WIDTH, BLOCK)
    in_specs = [
        pl.BlockSpec((1, WG_WIDTH, tq), lambda i, j: (i, QB0 // WG_WIDTH, j)),
        pl.BlockSpec(side_blk, lambda i, j: (i, kv_rows, jnp.maximum(j * WG_QB - 1, 0))),
        pl.BlockSpec((1, 2 * WG_KV_WIDTH, tq), lambda i, j: (i, kv_rows, j)),
        pl.BlockSpec(side_blk,
                     lambda i, j: (i, kv_rows, jnp.minimum((j + 1) * WG_QB, nblk - 1))),
        pl.BlockSpec((2 * WG_KV_WIDTH, N_META), lambda i, j: (0, 0)),
        pl.BlockSpec(band.shape, lambda i, j: (0, 0, 0)),
        pl.BlockSpec(bmeta.shape, lambda i, j: (0, 0, 0, 0)),
        pl.BlockSpec(sink_row.shape, lambda i, j: (0, 0, 0)),
        pl.BlockSpec((WG_WIDTH, 1), lambda i, j: (0, 0)),
    ]
    return pl.pallas_call(
        functools.partial(_wg_kernel, n_steps=n_steps),
        grid=(b, n_steps),
        in_specs=in_specs,
        out_specs=pl.BlockSpec((1, WG_WIDTH, tq), lambda i, j: (i, 0, j)),
        out_shape=jax.ShapeDtypeStruct((b, WG_WIDTH, n), jnp.bfloat16),
        scratch_shapes=[pltpu.VMEM((WG_WIDTH, tq), jnp.float32)],
        compiler_params=_params(2),
        name="wg_attn",
    )(proj_t, proj_t, proj_t, proj_t, kvm_t, band, bmeta, sink_row, g_col)


def _out_ffn_kernel(ma_ref, mb_ref, x_ref, woa_ref, wob_ref, g2_ref,
                    wg_ref, wu_ref, wd_ref, o_ref):
    n_tiles = x_ref.shape[1] // FFN_TILE
    n_chunks = D_FF // FFN_FC

    def attn_norm(t):
        ts = slice(t * FFN_TILE, (t + 1) * FFN_TILE)
        attn = lax.dot_general(ma_ref[0, :, ts], woa_ref[...], _TN,
                               preferred_element_type=jnp.float32)
        attn = _na_query_order(attn, inverse=True)
        attn = attn + lax.dot_general(mb_ref[0, :, ts], wob_ref[...], _TN,
                                      preferred_element_type=jnp.float32)
        x1 = x_ref[0, ts, :] + attn
        ms = jnp.mean(x1 * x1, axis=-1, keepdims=True)
        return x1, (x1 * lax.rsqrt(ms + EPS) * g2_ref[...]).astype(jnp.bfloat16)

    def gate_up(h, c):
        cs = slice(c * FFN_FC, (c + 1) * FFN_FC)
        return (jnp.dot(h, wg_ref[:, cs], preferred_element_type=jnp.float32),
                jnp.dot(h, wu_ref[:, cs], preferred_element_type=jnp.float32))

    items = [(t, c) for t in range(n_tiles) for c in range(n_chunks)]
    x1, h, acc = {}, {}, {}
    x1[0], h[0] = attn_norm(0)
    nxt = gate_up(h[0], 0)
    for idx, (t, c) in enumerate(items):
        gate, up = nxt
        if c == n_chunks - FFN_LOOKAHEAD and t + 1 < n_tiles:
            x1[t + 1], h[t + 1] = attn_norm(t + 1)
        if idx + 1 < len(items):
            t_next, c_next = items[idx + 1]
            nxt = gate_up(h[t_next], c_next)
        act = (gate * jax.nn.sigmoid(gate) * up).astype(jnp.bfloat16)
        down = jnp.dot(act, wd_ref[c * FFN_FC:(c + 1) * FFN_FC, :],
                       preferred_element_type=jnp.float32)
        acc[t] = down if c == 0 else acc[t] + down
        if c == n_chunks - 1:
            o_ref[0, t * FFN_TILE:(t + 1) * FFN_TILE, :] = x1[t] + acc[t]


def _out_ffn(mix_a, mix_b, x, wo_a, wo_b, g2, w_gate, w_up, w_down, tm):
    b, n, _ = x.shape

    def const(shape):
        return pl.BlockSpec(shape, lambda i, j: (0,) * len(shape),
                            pipeline_mode=pl.Buffered(1))

    mix_spec = pl.BlockSpec((1, NA_WIDTH, tm), lambda i, j: (i, 0, j))
    x_spec = pl.BlockSpec((1, tm, D_MODEL), lambda i, j: (i, j, 0))
    return pl.pallas_call(
        _out_ffn_kernel,
        grid=(b, n // tm),
        in_specs=[mix_spec, mix_spec, x_spec,
                  const((NA_WIDTH, D_MODEL)), const((WG_WIDTH, D_MODEL)),
                  const((1, D_MODEL)),
                  const((D_MODEL, D_FF)), const((D_MODEL, D_FF)),
                  const((D_FF, D_MODEL))],
        out_specs=x_spec,
        out_shape=jax.ShapeDtypeStruct((b, n, D_MODEL), jnp.float32),
        compiler_params=_params(2),
        name="out_ffn",
    )(mix_a, mix_b, x, wo_a, wo_b, g2, w_gate, w_up, w_down)


def _t5_bucket(rel):
    half = T5_BUCKETS // 2
    max_exact = half // 2
    ret = jnp.where(rel > 0, half, 0)
    n = jnp.abs(rel)
    nf = jnp.maximum(n, 1).astype(jnp.float32)
    large = max_exact + jnp.trunc(jnp.log(nf / max_exact) / math.log(T5_MAX_DIST / max_exact)
                                  * (half - max_exact)).astype(jnp.int32)
    large = jnp.minimum(large, half - 1)
    return ret + jnp.where(n < max_exact, n, large)


def _na_bias_table(rpb, variants):
    n_var, n_j, n_rr = variants.shape
    n_dcol = 2 * NA_WIN_COLS - 1
    sel_col = np.zeros((2, NA_KSPAN, NA_HALF, n_dcol), np.float32)
    for half in range(2):
        for k in range(NA_KSPAN):
            for cc in range(NA_HALF):
                kc, c = half * NA_KOFF + k, half * NA_HALF + cc
                cs = min(max(c - NA_WIN_COLS // 2, 0), GRID_W - NA_WIN_COLS)
                if cs <= kc < cs + NA_WIN_COLS:
                    sel_col[half, k, cc, kc - c + NA_WIN_COLS - 1] = 1.0
    sel_col = np.tile(sel_col, (1, 1, n_rr, 1))
    by_col = jnp.einsum("hde,akle->hadkl", rpb.astype(jnp.float32), sel_col,
                        precision=lax.Precision.HIGHEST)
    by_col = jnp.where(sel_col.sum(-1)[None, :, None] > 0, by_col, NEG_INF)
    by_col = jnp.concatenate([by_col, jnp.full_like(by_col[:, :, :1], NEG_INF)], axis=2)
    return pl.pallas_call(
        functools.partial(_na_table_kernel, variants=variants),
        grid=(NA_HEADS,),
        in_specs=[pl.BlockSpec((1,) + by_col.shape[1:], lambda h: (h, 0, 0, 0, 0))],
        out_specs=pl.BlockSpec((n_var, 1, n_j, 2, NA_KSPAN, n_rr * NA_HALF),
                               lambda h: (0, h, 0, 0, 0, 0)),
        out_shape=jax.ShapeDtypeStruct((n_var, NA_HEADS, n_j, 2, NA_KSPAN, n_rr * NA_HALF),
                                       jnp.float32),
        name="na_table",
    )(by_col)


def _na_table_kernel(by_col_ref, o_ref, *, variants):
    n_var, n_j, n_rr = variants.shape
    lane = lax.broadcasted_iota(jnp.int32, (NA_KSPAN, n_rr * NA_HALF), 1)
    for v in range(n_var):
        for j in range(n_j):
            for a in range(2):
                slab = by_col_ref[0, a, int(variants[v, j, n_rr - 1])]
                for rr in reversed(range(n_rr - 1)):
                    slab = jnp.where(lane < (rr + 1) * NA_HALF,
                                     by_col_ref[0, a, int(variants[v, j, rr])], slab)
                o_ref[v, 0, j, a] = slab


_T5_LANES = 4 * BLOCK


def _t5_layout_kernel(band_vec_ref, meta_vec_ref, band_ref, bmeta_ref):
    for h in range(WG_HEADS):
        g, hh = divmod(h, WG_GROUP)
        lanes = slice(hh * BLOCK, (hh + 1) * BLOCK)
        x = jnp.broadcast_to(band_vec_ref[h:h + 1, :], (3 * BLOCK, _T5_LANES))
        x = pltpu.roll(x, _T5_LANES - (3 * BLOCK - 1), 1, stride=1, stride_axis=0)
        band_ref[g, :, lanes] = x[:, :BLOCK]
        y = jnp.broadcast_to(meta_vec_ref[h:h + 1, :], (N_META, _T5_LANES))
        y = pltpu.roll(y, _T5_LANES - (N_META - 1), 1, stride=1, stride_axis=0)
        for s in range(2):
            bmeta_ref[g, s, :, lanes] = y[:, s * BLOCK:(s + 1) * BLOCK]


def _wg_bias_tables(t5):
    assert BLOCK + 1 >= T5_MAX_DIST
    rel = 2 * BLOCK - 1 - jnp.arange(_T5_LANES)
    band_vec = jnp.where((jnp.abs(rel) <= WINDOW)[:, None], t5[_t5_bucket(rel)], NEG_INF)
    meta_vec = t5[_t5_bucket(-1 - jnp.arange(_T5_LANES))]
    return pl.pallas_call(
        _t5_layout_kernel,
        out_shape=(
            jax.ShapeDtypeStruct((WG_KV_HEADS, 3 * BLOCK, WG_GROUP * BLOCK), jnp.float32),
            jax.ShapeDtypeStruct((WG_KV_HEADS, 2, N_META, WG_GROUP * BLOCK), jnp.float32)),
        name="t5_layout",
    )(band_vec.T, meta_vec.T)


def _encode(x, meta_proj, w_in_t, g1, gcol, na_bias, na_variants, ga_col, band, bmeta, sink_row,
            gb_col, wo_a, wo_b, g2, w_gate, w_up, w_down):
    proj_t = _in_proj(x, g1, w_in_t, gcol, PROJ_TM, PROJ_TILE)
    mix_a = _na_attn(proj_t, meta_proj[KA0:KA0 + 2 * NA_WIDTH], na_bias, na_variants, ga_col)
    mix_b = _wg_attn(proj_t, meta_proj[KB0:KB0 + 2 * WG_KV_WIDTH], band, bmeta, sink_row, gb_col)
    return _out_ffn(mix_a, mix_b, x, wo_a, wo_b, g2, w_gate, w_up, w_down, FFN_TM)


def kernel(x_prompt, x_sample, meta_tokens, t5_table, norm1_g, w_in, qn_a_g, kn_a_g, rpb_a,
           qn_b_g, kn_b_g, sink_b, outn_a_g, outn_b_g, w_out, norm2_g, w_gate, w_up, w_down):
    f32, bf16 = jnp.float32, jnp.bfloat16
    q_scale = HEAD_DIM ** -0.5 * LOG2E
    w_in_t = w_in[0].astype(bf16)
    g1 = norm1_g[0].astype(f32).reshape(1, D_MODEL)
    g2 = norm2_g[0].astype(f32).reshape(1, D_MODEL)
    ones = jnp.ones
    gcol = jnp.concatenate([
        jnp.tile(qn_a_g[0].astype(f32), NA_HEADS) * q_scale,
        jnp.tile(qn_b_g[0].astype(f32), WG_HEADS) * q_scale,
        jnp.tile(kn_a_g[0].astype(f32), NA_HEADS),
        ones((NA_WIDTH,), f32),
        jnp.tile(kn_b_g[0].astype(f32), WG_KV_HEADS),
        ones((WG_KV_WIDTH,), f32),
    ]).reshape(IN_WIDTH, 1)
    ga_col = outn_a_g[0].astype(f32).reshape(NA_WIDTH, 1)
    gb_col = outn_b_g[0].astype(f32).reshape(WG_WIDTH, 1)
    t5 = t5_table.astype(f32) * LOG2E
    na_variants = _na_variants(x_prompt.shape[1] // GRID_W)
    na_bias = _na_bias_table(rpb_a[0].astype(f32) * LOG2E, na_variants)
    band, bmeta = _wg_bias_tables(t5)
    sink_row = jnp.repeat(sink_b[0].astype(f32).reshape(WG_KV_HEADS, WG_GROUP) * LOG2E, BLOCK,
                          axis=1).reshape(WG_KV_HEADS, 1, WG_GROUP * BLOCK)
    wo = w_out[0].astype(bf16)
    wo_a, wo_b = wo[:NA_WIDTH], wo[NA_WIDTH:]
    wg, wu, wd = w_gate[0].astype(bf16), w_up[0].astype(bf16), w_down[0].astype(bf16)

    meta = jnp.zeros((1, META_PAD, D_MODEL), f32).at[0, :N_META].set(meta_tokens.astype(f32))
    meta_proj = _in_proj(meta, g1, w_in_t, gcol, META_PAD, META_PAD)[0, :, :N_META]

    def enc(x):
        return _encode(x, meta_proj, w_in_t, g1, gcol, na_bias, na_variants, ga_col, band, bmeta,
                       sink_row, gb_col, wo_a, wo_b, g2, wg, wu, wd)

    return (enc(x_prompt), enc(x_sample))
```

```python
import functools
import math

import jax
import jax.numpy as jnp
import numpy as np
from jax import lax
from jax.experimental import pallas as pl
from jax.experimental.pallas import tpu as pltpu

D_MODEL = 1024
HEAD_DIM = 64
NA_HEADS = 8
WG_HEADS = 8
WG_KV_HEADS = 2
WG_GROUP = WG_HEADS // WG_KV_HEADS
NA_WIDTH = NA_HEADS * HEAD_DIM
WG_WIDTH = WG_HEADS * HEAD_DIM
WG_KV_WIDTH = WG_KV_HEADS * HEAD_DIM
IN_WIDTH = 3 * NA_WIDTH + WG_WIDTH + 2 * WG_KV_WIDTH
D_FF = 2816
GRID_W = 64
NA_WIN_ROWS = 8
NA_WIN_COLS = 16
N_META = 16
WINDOW = 128
BLOCK = 128
T5_BUCKETS = 32
T5_MAX_DIST = 128
EPS = 1e-6
NEG_INF = -1e30

QA0 = 0
QB0 = QA0 + NA_WIDTH
KA0 = QB0 + WG_WIDTH
VA0 = KA0 + NA_WIDTH
KB0 = VA0 + NA_WIDTH
VB0 = KB0 + WG_KV_WIDTH
LOG2E = math.log2(math.e)

NA_UNIT_ROWS = 4
NA_UNIT = NA_UNIT_ROWS * GRID_W
NA_KEY_UNITS = 3
NA_STEP_UNITS = 2
NA_NEG_SLAB = 2 * NA_WIN_ROWS - 1
NA_HALF = GRID_W // 2
NA_KSPAN = NA_HALF + NA_WIN_COLS // 2
NA_KOFF = GRID_W - NA_KSPAN
WG_QB = 8

PROJ_TM = 2048
PROJ_TILE = 256
FFN_TM = 1024
FFN_TILE = 256
FFN_FC = 256
FFN_LOOKAHEAD = 4
META_PAD = 128

VMEM_LIMIT_BYTES = 56 * 1024 * 1024

_TN = (((0,), (0,)), ((), ()))
_NT = (((1,), (1,)), ((), ()))
_NN = (((1,), (0,)), ((), ()))


def _params(n_grid_dims):
    return pltpu.CompilerParams(
        dimension_semantics=("arbitrary",) * n_grid_dims,
        vmem_limit_bytes=VMEM_LIMIT_BYTES)


_PROJ_CHUNKS = (
    (QA0, 0, NA_WIDTH, NA_WIDTH, True),
    (QB0, 3 * NA_WIDTH, WG_WIDTH, WG_WIDTH, False),
    (KA0, NA_WIDTH, NA_WIDTH, NA_WIDTH, False),
    (VA0, 2 * NA_WIDTH, NA_WIDTH, 0, False),
    (KB0, 3 * NA_WIDTH + WG_WIDTH, 2 * WG_KV_WIDTH, WG_KV_WIDTH, False),
)


def _na_query_order(a, inverse=False):
    chunks = []
    for u in range(a.shape[0] // NA_UNIT):
        for outer in range(NA_UNIT_ROWS if inverse else 2):
            for inner in range(2 if inverse else NA_UNIT_ROWS):
                rr, half = (outer, inner) if inverse else (inner, outer)
                src = (half * NA_UNIT_ROWS + rr) if inverse else (rr * 2 + half)
                r0 = u * NA_UNIT + src * NA_HALF
                chunks.append(a[r0:r0 + NA_HALF])
    return jnp.concatenate(chunks, axis=0)


def _in_proj_kernel(x_ref, g1_ref, w_ref, gcol_ref, o_ref, *, tile):
    n_tiles = x_ref.shape[1] // tile

    def norm(t):
        x = x_ref[0, t * tile:(t + 1) * tile, :]
        ms = jnp.mean(x * x, axis=-1, keepdims=True)
        h = (x * lax.rsqrt(ms + EPS) * g1_ref[...]).astype(jnp.bfloat16)
        return h, (_na_query_order(h) if tile % NA_UNIT == 0 else h)

    def project(hs, chunk):
        _, col0, rows, _, na_order = chunk
        p = jnp.dot(hs[1] if na_order else hs[0], w_ref[:, col0:col0 + rows],
                    preferred_element_type=jnp.float32)
        return p.T

    def head_norm_store(t, chunk, p):
        row0, _, rows, normed_rows, _ = chunk
        ts = slice(t * tile, (t + 1) * tile)
        if normed_rows < rows:
            o_ref[0, row0 + normed_rows:row0 + rows, ts] = p[normed_rows:].astype(jnp.bfloat16)
        for i in range(normed_rows // HEAD_DIM):
            blk = p[i * HEAD_DIM:(i + 1) * HEAD_DIM, :]
            ss = jnp.mean(blk * blk, axis=0, keepdims=True)
            r0 = row0 + i * HEAD_DIM
            y = blk * lax.rsqrt(ss + EPS) * gcol_ref[r0:r0 + HEAD_DIM, :]
            o_ref[0, r0:r0 + HEAD_DIM, ts] = y.astype(jnp.bfloat16)

    items = [(t, chunk) for t in range(n_tiles) for chunk in _PROJ_CHUNKS]
    hs = {0: norm(0)}
    nxt = project(hs[0], items[0][1])
    for idx, (t, chunk) in enumerate(items):
        p = nxt
        if chunk is _PROJ_CHUNKS[-3] and t + 1 < n_tiles:
            hs[t + 1] = norm(t + 1)
        if idx + 1 < len(items):
            t_next, chunk_next = items[idx + 1]
            nxt = project(hs[t_next], chunk_next)
        head_norm_store(t, chunk, p)


def _in_proj(x, g1, w_t, gcol, tm, tile):
    b, n, _ = x.shape
    assert n % tm == 0 and tm % tile == 0
    return pl.pallas_call(
        functools.partial(_in_proj_kernel, tile=tile),
        grid=(b, n // tm),
        in_specs=[
            pl.BlockSpec((1, tm, D_MODEL), lambda i, j: (i, j, 0)),
            pl.BlockSpec((1, D_MODEL), lambda i, j: (0, 0)),
            pl.BlockSpec((D_MODEL, IN_WIDTH), lambda i, j: (0, 0),
                         pipeline_mode=pl.Buffered(1)),
            pl.BlockSpec((IN_WIDTH, 1), lambda i, j: (0, 0)),
        ],
        out_specs=pl.BlockSpec((1, IN_WIDTH, tm), lambda i, j: (i, 0, j)),
        out_shape=jax.ShapeDtypeStruct((b, IN_WIDTH, n), jnp.bfloat16),
        compiler_params=_params(2),
        name="in_proj",
    )(x, g1, w_t, gcol)


def _fold8(x, op):
    acc = x[0:8]
    for r in range(8, x.shape[0], 8):
        acc = op(acc, x[r:r + 8])
    return acc


def _na_window(step, n_units):
    lo = jnp.clip(NA_STEP_UNITS * step - 1, 0, n_units - (NA_STEP_UNITS + 2))
    starts = [jnp.clip(NA_STEP_UNITS * step + x - 1, 0, n_units - NA_KEY_UNITS) - lo
              for x in range(NA_STEP_UNITS)]
    return lo, starts


def _na_kernel(q_ref, kv0_ref, kv1_ref, kv2_ref, kv3_ref, kvm_ref, t0_ref, t1_ref,
               g_ref, o_ref, acc_ref, *, n_units):
    kv_refs = (kv0_ref, kv1_ref, kv2_ref, kv3_ref)
    t_refs = (t0_ref, t1_ref)
    lanes = (slice(0, 2 * GRID_W), slice(2 * GRID_W, 4 * GRID_W))
    _, starts = _na_window(pl.program_id(1), n_units)

    def kv_block(x, c, rows):
        return jnp.where(starts[x] == 1, kv_refs[c + 1][0, rows, :], kv_refs[c][0, rows, :])

    def scores(x, h):
        hs = slice(h * HEAD_DIM, (h + 1) * HEAD_DIM)
        q = q_ref[0, hs, x * NA_UNIT:(x + 1) * NA_UNIT]
        blocks = []
        for c in range(NA_KEY_UNITS):
            s = lax.dot_general(kv_block(x, c, hs), q, _TN,
                                preferred_element_type=jnp.float32)
            for jj in range(NA_UNIT_ROWS):
                j = NA_UNIT_ROWS * c + jj
                pair = []
                for half in range(2):
                    r0 = jj * GRID_W + half * NA_KOFF
                    pair.append(s[r0:r0 + NA_KSPAN, lanes[half]] + t_refs[x][0, h, j, half])
                blocks.append(pair)
        s_meta = lax.dot_general(kvm_ref[hs, :], q, _TN,
                                 preferred_element_type=jnp.float32)
        return blocks, s_meta

    def finish(x, h, blocks, s_meta):
        hs = slice(h * HEAD_DIM, (h + 1) * HEAD_DIM)
        vs = slice(NA_WIDTH + h * HEAD_DIM, NA_WIDTH + (h + 1) * HEAD_DIM)
        m8 = _fold8(s_meta, jnp.maximum)
        m8 = [m8[:, lanes[0]], m8[:, lanes[1]]]
        for pair in blocks:
            for half in range(2):
                m8[half] = jnp.maximum(m8[half], _fold8(pair[half], jnp.maximum))
        m = [jnp.max(m8[half], axis=0, keepdims=True) for half in range(2)]
        p_meta = jnp.exp2(s_meta - jnp.concatenate(m, axis=1))
        l8 = _fold8(p_meta, jnp.add)
        l8 = [l8[:, lanes[0]], l8[:, lanes[1]]]
        pad = jnp.zeros((NA_KOFF, 2 * GRID_W), jnp.float32)
        p_rows, values = [], []
        for c in range(NA_KEY_UNITS):
            values.append(kv_block(x, c, vs))
            for jj in range(NA_UNIT_ROWS):
                pair = blocks[NA_UNIT_ROWS * c + jj]
                p = [jnp.exp2(pair[half] - m[half]) for half in range(2)]
                for half in range(2):
                    l8[half] = l8[half] + _fold8(p[half], jnp.add)
                p_rows.append(jnp.concatenate(
                    [jnp.concatenate([p[0], pad], axis=0),
                     jnp.concatenate([pad, p[1]], axis=0)], axis=1))
        o = lax.dot_general(
            jnp.concatenate(values + [kvm_ref[vs, :]], axis=1),
            jnp.concatenate(p_rows + [p_meta], axis=0).astype(jnp.bfloat16),
            _NN, preferred_element_type=jnp.float32)
        l = jnp.concatenate([jnp.sum(l8[half], axis=0, keepdims=True) for half in range(2)],
                            axis=1)
        acc_ref[hs, x * NA_UNIT:(x + 1) * NA_UNIT] = o * (1.0 / l)

    items = [(x, h) for x in range(NA_STEP_UNITS) for h in range(NA_HEADS)]
    nxt = scores(*items[0])
    for i, item in enumerate(items):
        cur = nxt
        if i + 1 < len(items):
            nxt = scores(*items[i + 1])
        finish(*item, *cur)

    full = acc_ref[...]
    ms = jnp.mean(full * full, axis=0, keepdims=True)
    o_ref[0] = (full * lax.rsqrt(ms + EPS) * g_ref[...]).astype(jnp.bfloat16)


def _na_slab_index(rows, u):
    n_units = rows // NA_UNIT_ROWS
    k_row0 = NA_UNIT_ROWS * min(max(u - 1, 0), n_units - NA_KEY_UNITS)
    idx = np.full((NA_KEY_UNITS * NA_UNIT_ROWS, NA_UNIT_ROWS), NA_NEG_SLAB, np.int32)
    for j in range(idx.shape[0]):
        for rr in range(NA_UNIT_ROWS):
            kr, r = k_row0 + j, NA_UNIT_ROWS * u + rr
            rs = min(max(r - NA_WIN_ROWS // 2, 0), rows - NA_WIN_ROWS)
            if rs <= kr < rs + NA_WIN_ROWS:
                idx[j, rr] = kr - r + NA_WIN_ROWS - 1
    return idx


def _na_variant(u, n_units):
    return jnp.where(u == 0, 0, jnp.where(u == n_units - 1, 2, 1))


def _na_variants(rows):
    n_units = rows // NA_UNIT_ROWS
    assert rows % NA_UNIT_ROWS == 0 and n_units >= NA_KEY_UNITS
    variants = np.stack([_na_slab_index(rows, u) for u in (0, 1, n_units - 1)])
    for u in range(n_units):
        v = 0 if u == 0 else (2 if u == n_units - 1 else 1)
        assert (_na_slab_index(rows, u) == variants[v]).all()
    return variants


def _na_attn(proj_t, kvm_t, bias, variants, g_col):
    b, _, n = proj_t.shape
    n_units = n // NA_UNIT
    assert (_na_variants(n // GRID_W) == variants).all()
    assert NA_STEP_UNITS == 2 and n_units % NA_STEP_UNITS == 0 and n_units >= NA_STEP_UNITS + 2
    assert VA0 == KA0 + NA_WIDTH and KA0 % (2 * NA_WIDTH) == 0

    def kv_map(c):
        return lambda i, s: (i, KA0 // (2 * NA_WIDTH), _na_window(s, n_units)[0] + c)

    def bias_map(x):
        return lambda i, s: (_na_variant(NA_STEP_UNITS * s + x, n_units), 0, 0, 0, 0, 0)

    blk = (1, NA_WIDTH, NA_STEP_UNITS * NA_UNIT)
    in_specs = [pl.BlockSpec(blk, lambda i, s: (i, QA0 // NA_WIDTH, s))]
    in_specs += [pl.BlockSpec((1, 2 * NA_WIDTH, NA_UNIT), kv_map(c))
                 for c in range(NA_STEP_UNITS + 2)]
    in_specs += [pl.BlockSpec((2 * NA_WIDTH, N_META), lambda i, s: (0, 0))]
    in_specs += [pl.BlockSpec((1,) + bias.shape[1:], bias_map(x)) for x in range(NA_STEP_UNITS)]
    in_specs += [pl.BlockSpec((NA_WIDTH, 1), lambda i, s: (0, 0))]
    return pl.pallas_call(
        functools.partial(_na_kernel, n_units=n_units),
        grid=(b, n_units // NA_STEP_UNITS),
        in_specs=in_specs,
        out_specs=pl.BlockSpec(blk, lambda i, s: (i, 0, s)),
        out_shape=jax.ShapeDtypeStruct((b, NA_WIDTH, n), jnp.bfloat16),
        scratch_shapes=[pltpu.VMEM((NA_WIDTH, NA_STEP_UNITS * NA_UNIT), jnp.float32)],
        compiler_params=_params(2),
        name="na_attn",
    )(proj_t, proj_t, proj_t, proj_t, proj_t, kvm_t, bias, bias, g_col)


def _wg_kernel(q_ref, kvl_ref, kvm_ref, kvr_ref, kvmeta_ref, band_ref, bmeta_ref,
               sink_ref, g_ref, o_ref, acc_ref, *, n_steps):
    j = pl.program_id(1)
    first = j == 0
    last = j == n_steps - 1

    def piece(rows, i):
        if i == 0:
            return kvl_ref[0, rows, :]
        if i == WG_QB + 1:
            return kvr_ref[0, rows, :]
        return kvm_ref[0, rows, (i - 1) * BLOCK:i * BLOCK]

    def scores(g, qb):
        gs = slice(g * HEAD_DIM, (g + 1) * HEAD_DIM)
        q = jnp.concatenate(
            [q_ref[0, (g * WG_GROUP + hh) * HEAD_DIM:(g * WG_GROUP + hh + 1) * HEAD_DIM,
                   qb * BLOCK:(qb + 1) * BLOCK]
             for hh in range(WG_GROUP)], axis=1)
        keys = jnp.concatenate([piece(gs, qb + c) for c in range(3)] + [kvmeta_ref[gs, :]],
                               axis=1)
        s_all = lax.dot_general(keys, q, _TN, preferred_element_type=jnp.float32)
        s_blocks = []
        for c in range(3):
            s = s_all[c * BLOCK:(c + 1) * BLOCK] + band_ref[g, c * BLOCK:(c + 1) * BLOCK, :]
            if c == 0 and qb == 0:
                s = s + jnp.where(first, NEG_INF, 0.0)
            if c == 2 and qb == WG_QB - 1:
                s = s + jnp.where(last, NEG_INF, 0.0)
            s_blocks.append(s)
        meta_slab = jnp.where(first, 0, 1) if qb == 0 else 1
        s_meta = s_all[3 * BLOCK:] + bmeta_ref[g, meta_slab]
        return s_blocks, s_meta

    def finish(g, qb, s_blocks, s_meta):
        vs = slice(WG_KV_WIDTH + g * HEAD_DIM, WG_KV_WIDTH + (g + 1) * HEAD_DIM)
        sink = sink_ref[g]
        m8 = _fold8(s_meta, jnp.maximum)
        for s in s_blocks:
            m8 = jnp.maximum(m8, _fold8(s, jnp.maximum))
        m = jnp.maximum(sink, jnp.max(m8, axis=0, keepdims=True))
        p_meta = jnp.exp2(s_meta - m)
        l8 = _fold8(p_meta, jnp.add)
        probs, values = [], []
        for c in range(3):
            p = jnp.exp2(s_blocks[c] - m)
            l8 = l8 + _fold8(p, jnp.add)
            probs.append(p.astype(jnp.bfloat16))
            values.append(piece(vs, qb + c))
        o = lax.dot_general(
            jnp.concatenate(values + [kvmeta_ref[vs, :]], axis=1),
            jnp.concatenate(probs + [p_meta.astype(jnp.bfloat16)], axis=0),
            _NN, preferred_element_type=jnp.float32)
        l = jnp.exp2(sink - m) + jnp.sum(l8, axis=0, keepdims=True)
        o = o * (1.0 / l)
        for hh in range(WG_GROUP):
            r0 = (g * WG_GROUP + hh) * HEAD_DIM
            acc_ref[r0:r0 + HEAD_DIM, qb * BLOCK:(qb + 1) * BLOCK] = (
                o[:, hh * BLOCK:(hh + 1) * BLOCK])

    units = [(g, qb) for qb in range(WG_QB) for g in range(WG_KV_HEADS)]
    nxt = scores(*units[0])
    for i, unit in enumerate(units):
        cur = nxt
        if i + 1 < len(units):
            nxt = scores(*units[i + 1])
        finish(*unit, *cur)

    full = acc_ref[...]
    ms = jnp.mean(full * full, axis=0, keepdims=True)
    o_ref[0] = (full * lax.rsqrt(ms + EPS) * g_ref[...]).astype(jnp.bfloat16)


def _wg_attn(proj_t, kvm_t, band, bmeta, sink_row, g_col):
    b, _, n = proj_t.shape
    nblk = n // BLOCK
    n_steps = nblk // WG_QB
    assert nblk % WG_QB == 0
    assert VB0 == KB0 + WG_KV_WIDTH and KB0 % (2 * WG_KV_WIDTH) == 0
    kv_rows = KB0 // (2 * WG_KV_WIDTH)

    tq = WG_QB * BLOCK
    side_blk = (1, 2 * WG_KV_WIDTH, BLOCK)
    in_specs = [
        pl.BlockSpec((1, WG_WIDTH, tq), lambda i, j: (i, QB0 // WG_WIDTH, j)),
        pl.BlockSpec(side_blk, lambda i, j: (i, kv_rows, jnp.maximum(j * WG_QB - 1, 0))),
        pl.BlockSpec((1, 2 * WG_KV_WIDTH, tq), lambda i, j: (i, kv_rows, j)),
        pl.BlockSpec(side_blk,
                     lambda i, j: (i, kv_rows, jnp.minimum((j + 1) * WG_QB, nblk - 1))),
        pl.BlockSpec((2 * WG_KV_WIDTH, N_META), lambda i, j: (0, 0)),
        pl.BlockSpec(band.shape, lambda i, j: (0, 0, 0)),
        pl.BlockSpec(bmeta.shape, lambda i, j: (0, 0, 0, 0)),
        pl.BlockSpec(sink_row.shape, lambda i, j: (0, 0, 0)),
        pl.BlockSpec((WG_WIDTH, 1), lambda i, j: (0, 0)),
    ]
    return pl.pallas_call(
        functools.partial(_wg_kernel, n_steps=n_steps),
        grid=(b, n_steps),
        in_specs=in_specs,
        out_specs=pl.BlockSpec((1, WG_WIDTH, tq), lambda i, j: (i, 0, j)),
        out_shape=jax.ShapeDtypeStruct((b, WG_WIDTH, n), jnp.bfloat16),
        scratch_shapes=[pltpu.VMEM((WG_WIDTH, tq), jnp.float32)],
        compiler_params=_params(2),
        name="wg_attn",
    )(proj_t, proj_t, proj_t, proj_t, kvm_t, band, bmeta, sink_row, g_col)


def _out_ffn_kernel(ma_ref, mb_ref, x_ref, woa_ref, wob_ref, g2_ref,
                    wg_ref, wu_ref, wd_ref, o_ref):
    n_tiles = x_ref.shape[1] // FFN_TILE
    n_chunks = D_FF // FFN_FC

    def attn_norm(t):
        ts = slice(t * FFN_TILE, (t + 1) * FFN_TILE)
        attn = lax.dot_general(ma_ref[0, :, ts], woa_ref[...], _TN,
                               preferred_element_type=jnp.float32)
        attn = _na_query_order(attn, inverse=True)
        attn = attn + lax.dot_general(mb_ref[0, :, ts], wob_ref[...], _TN,
                                      preferred_element_type=jnp.float32)
        x1 = x_ref[0, ts, :] + attn
        ms = jnp.mean(x1 * x1, axis=-1, keepdims=True)
        return x1, (x1 * lax.rsqrt(ms + EPS) * g2_ref[...]).astype(jnp.bfloat16)

    def gate_up(h, c):
        cs = slice(c * FFN_FC, (c + 1) * FFN_FC)
        return (jnp.dot(h, wg_ref[:, cs], preferred_element_type=jnp.float32),
                jnp.dot(h, wu_ref[:, cs], preferred_element_type=jnp.float32))

    items = [(t, c) for t in range(n_tiles) for c in range(n_chunks)]
    x1, h, acc = {}, {}, {}
    x1[0], h[0] = attn_norm(0)
    nxt = gate_up(h[0], 0)
    for idx, (t, c) in enumerate(items):
        gate, up = nxt
        if c == n_chunks - FFN_LOOKAHEAD and t + 1 < n_tiles:
            x1[t + 1], h[t + 1] = attn_norm(t + 1)
        if idx + 1 < len(items):
            t_next, c_next = items[idx + 1]
            nxt = gate_up(h[t_next], c_next)
        act = (gate * jax.nn.sigmoid(gate) * up).astype(jnp.bfloat16)
        down = jnp.dot(act, wd_ref[c * FFN_FC:(c + 1) * FFN_FC, :],
                       preferred_element_type=jnp.float32)
        acc[t] = down if c == 0 else acc[t] + down
        if c == n_chunks - 1:
            o_ref[0, t * FFN_TILE:(t + 1) * FFN_TILE, :] = x1[t] + acc[t]


def _out_ffn(mix_a, mix_b, x, wo_a, wo_b, g2, w_gate, w_up, w_down, tm):
    b, n, _ = x.shape

    def const(shape):
        return pl.BlockSpec(shape, lambda i, j: (0,) * len(shape),
                            pipeline_mode=pl.Buffered(1))

    mix_spec = pl.BlockSpec((1, NA_WIDTH, tm), lambda i, j: (i, 0, j))
    x_spec = pl.BlockSpec((1, tm, D_MODEL), lambda i, j: (i, j, 0))
    return pl.pallas_call(
        _out_ffn_kernel,
        grid=(b, n // tm),
        in_specs=[mix_spec, mix_spec, x_spec,
                  const((NA_WIDTH, D_MODEL)), const((WG_WIDTH, D_MODEL)),
                  const((1, D_MODEL)),
                  const((D_MODEL, D_FF)), const((D_MODEL, D_FF)),
                  const((D_FF, D_MODEL))],
        out_specs=x_spec,
        out_shape=jax.ShapeDtypeStruct((b, n, D_MODEL), jnp.float32),
        compiler_params=_params(2),
        name="out_ffn",
    )(mix_a, mix_b, x, wo_a, wo_b, g2, w_gate, w_up, w_down)


def _t5_bucket(rel):
    half = T5_BUCKETS // 2
    max_exact = half // 2
    ret = jnp.where(rel > 0, half, 0)
    n = jnp.abs(rel)
    nf = jnp.maximum(n, 1).astype(jnp.float32)
    large = max_exact + jnp.trunc(jnp.log(nf / max_exact) / math.log(T5_MAX_DIST / max_exact)
                                  * (half - max_exact)).astype(jnp.int32)
    large = jnp.minimum(large, half - 1)
    return ret + jnp.where(n < max_exact, n, large)


def _na_bias_table(rpb, variants):
    n_var, n_j, n_rr = variants.shape
    n_dcol = 2 * NA_WIN_COLS - 1
    sel_col = np.zeros((2, NA_KSPAN, NA_HALF, n_dcol), np.float32)
    for half in range(2):
        for k in range(NA_KSPAN):
            for cc in range(NA_HALF):
                kc, c = half * NA_KOFF + k, half * NA_HALF + cc
                cs = min(max(c - NA_WIN_COLS // 2, 0), GRID_W - NA_WIN_COLS)
                if cs <= kc < cs + NA_WIN_COLS:
                    sel_col[half, k, cc, kc - c + NA_WIN_COLS - 1] = 1.0
    sel_col = np.tile(sel_col, (1, 1, n_rr, 1))
    by_col = jnp.einsum("hde,akle->hadkl", rpb.astype(jnp.float32), sel_col,
                        precision=lax.Precision.HIGHEST)
    by_col = jnp.where(sel_col.sum(-1)[None, :, None] > 0, by_col, NEG_INF)
    by_col = jnp.concatenate([by_col, jnp.full_like(by_col[:, :, :1], NEG_INF)], axis=2)
    return pl.pallas_call(
        functools.partial(_na_table_kernel, variants=variants),
        grid=(NA_HEADS,),
        in_specs=[pl.BlockSpec((1,) + by_col.shape[1:], lambda h: (h, 0, 0, 0, 0))],
        out_specs=pl.BlockSpec((n_var, 1, n_j, 2, NA_KSPAN, n_rr * NA_HALF),
                               lambda h: (0, h, 0, 0, 0, 0)),
        out_shape=jax.ShapeDtypeStruct((n_var, NA_HEADS, n_j, 2, NA_KSPAN, n_rr * NA_HALF),
                                       jnp.float32),
        name="na_table",
    )(by_col)


def _na_table_kernel(by_col_ref, o_ref, *, variants):
    n_var, n_j, n_rr = variants.shape
    lane = lax.broadcasted_iota(jnp.int32, (NA_KSPAN, n_rr * NA_HALF), 1)
    for v in range(n_var):
        for j in range(n_j):
            for a in range(2):
                slab = by_col_ref[0, a, int(variants[v, j, n_rr - 1])]
                for rr in reversed(range(n_rr - 1)):
                    slab = jnp.where(lane < (rr + 1) * NA_HALF,
                                     by_col_ref[0, a, int(variants[v, j, rr])], slab)
                o_ref[v, 0, j, a] = slab


_T5_LANES = 4 * BLOCK


def _t5_layout_kernel(band_vec_ref, meta_vec_ref, band_ref, bmeta_ref):
    for h in range(WG_HEADS):
        g, hh = divmod(h, WG_GROUP)
        lanes = slice(hh * BLOCK, (hh + 1) * BLOCK)
        x = jnp.broadcast_to(band_vec_ref[h:h + 1, :], (3 * BLOCK, _T5_LANES))
        x = pltpu.roll(x, _T5_LANES - (3 * BLOCK - 1), 1, stride=1, stride_axis=0)
        band_ref[g, :, lanes] = x[:, :BLOCK]
        y = jnp.broadcast_to(meta_vec_ref[h:h + 1, :], (N_META, _T5_LANES))
        y = pltpu.roll(y, _T5_LANES - (N_META - 1), 1, stride=1, stride_axis=0)
        for s in range(2):
            bmeta_ref[g, s, :, lanes] = y[:, s * BLOCK:(s + 1) * BLOCK]


def _wg_bias_tables(t5):
    assert BLOCK + 1 >= T5_MAX_DIST
    rel = 2 * BLOCK - 1 - jnp.arange(_T5_LANES)
    band_vec = jnp.where((jnp.abs(rel) <= WINDOW)[:, None], t5[_t5_bucket(rel)], NEG_INF)
    meta_vec = t5[_t5_bucket(-1 - jnp.arange(_T5_LANES))]
    return pl.pallas_call(
        _t5_layout_kernel,
        out_shape=(
            jax.ShapeDtypeStruct((WG_KV_HEADS, 3 * BLOCK, WG_GROUP * BLOCK), jnp.float32),
            jax.ShapeDtypeStruct((WG_KV_HEADS, 2, N_META, WG_GROUP * BLOCK), jnp.float32)),
        name="t5_layout",
    )(band_vec.T, meta_vec.T)


def _encode(x, meta_proj, w_in_t, g1, gcol, na_bias, na_variants, ga_col, band, bmeta, sink_row,
            gb_col, wo_a, wo_b, g2, w_gate, w_up, w_down):
    proj_t = _in_proj(x, g1, w_in_t, gcol, PROJ_TM, PROJ_TILE)
    mix_a = _na_attn(proj_t, meta_proj[KA0:KA0 + 2 * NA_WIDTH], na_bias, na_variants, ga_col)
    mix_b = _wg_attn(proj_t, meta_proj[KB0:KB0 + 2 * WG_KV_WIDTH], band, bmeta, sink_row, gb_col)
    return _out_ffn(mix_a, mix_b, x, wo_a, wo_b, g2, w_gate, w_up, w_down, FFN_TM)


def kernel(x_prompt, x_sample, meta_tokens, t5_table, norm1_g, w_in, qn_a_g, kn_a_g, rpb_a,
           qn_b_g, kn_b_g, sink_b, outn_a_g, outn_b_g, w_out, norm2_g, w_gate, w_up, w_down):
    f32, bf16 = jnp.float32, jnp.bfloat16
    q_scale = HEAD_DIM ** -0.5 * LOG2E
    w_in_t = w_in[0].astype(bf16)
    g1 = norm1_g[0].astype(f32).reshape(1, D_MODEL)
    g2 = norm2_g[0].astype(f32).reshape(1, D_MODEL)
    ones = jnp.ones
    gcol = jnp.concatenate([
        jnp.tile(qn_a_g[0].astype(f32), NA_HEADS) * q_scale,
        jnp.tile(qn_b_g[0].astype(f32), WG_HEADS) * q_scale,
        jnp.tile(kn_a_g[0].astype(f32), NA_HEADS),
        ones((NA_WIDTH,), f32),
        jnp.tile(kn_b_g[0].astype(f32), WG_KV_HEADS),
        ones((WG_KV_WIDTH,), f32),
    ]).reshape(IN_WIDTH, 1)
    ga_col = outn_a_g[0].astype(f32).reshape(NA_WIDTH, 1)
    gb_col = outn_b_g[0].astype(f32).reshape(WG_WIDTH, 1)
    t5 = t5_table.astype(f32) * LOG2E
    na_variants = _na_variants(x_prompt.shape[1] // GRID_W)
    na_bias = _na_bias_table(rpb_a[0].astype(f32) * LOG2E, na_variants)
    band, bmeta = _wg_bias_tables(t5)
    sink_row = jnp.repeat(sink_b[0].astype(f32).reshape(WG_KV_HEADS, WG_GROUP) * LOG2E, BLOCK,
                          axis=1).reshape(WG_KV_HEADS, 1, WG_GROUP * BLOCK)
    wo = w_out[0].astype(bf16)
    wo_a, wo_b = wo[:NA_WIDTH], wo[NA_WIDTH:]
    wg, wu, wd = w_gate[0].astype(bf16), w_up[0].astype(bf16), w_down[0].astype(bf16)

    meta = jnp.zeros((1, META_PAD, D_MODEL), f32).at[0, :N_META].set(meta_tokens.astype(f32))
    meta_proj = _in_proj(meta, g1, w_in_t, gcol, META_PAD, META_PAD)[0, :, :N_META]

    def enc(x):
        return _encode(x, meta_proj, w_in_t, g1, gcol, na_bias, na_variants, ga_col, band, bmeta,
                       sink_row, gb_col, wo_a, wo_b, g2, wg, wu, wd)

    return (enc(x_prompt), enc(x_sample))
```
